```python
import math
import jax
import jax.numpy as jnp
from jax import lax
import numpy as np

D_MODEL = 4096
BATCH = 1
SEQ = 16384
DEPTH = 4

CHUNK = 64
N_MIXERS = 2
N_SSD_LAYERS = (DEPTH + N_MIXERS - 1) // N_MIXERS
N_DSA_LAYERS = DEPTH // N_MIXERS

SSD_EXPAND = 2
SSD_D_INNER = SSD_EXPAND * D_MODEL
SSD_HEAD_DIM = 64
SSD_N_HEADS = SSD_D_INNER // SSD_HEAD_DIM
SSD_D_STATE = 128
SSD_N_GROUPS = 8
SSD_HEADS_PER_GROUP = SSD_N_HEADS // SSD_N_GROUPS
SSD_CONV = 4
SSD_CHUNK = 128
SSD_CONV_DIM = SSD_D_INNER + 2 * SSD_N_GROUPS * SSD_D_STATE
SSD_IN_DIM = SSD_D_INNER + SSD_CONV_DIM + SSD_N_HEADS

ATT_HEAD_DIM = 128
ATT_N_HEADS = D_MODEL // ATT_HEAD_DIM
ATT_N_KV = 8
ATT_GROUP = ATT_N_HEADS // ATT_N_KV
ATT_WIDTH = ATT_N_HEADS * ATT_HEAD_DIM
IDX_N_HEADS = 64
IDX_HEAD_DIM = 128
TOPK_MAX = 256
Q_BLOCK = 128
DSA_SPLITS = (ATT_WIDTH, ATT_N_KV * ATT_HEAD_DIM, ATT_N_KV * ATT_HEAD_DIM, ATT_WIDTH,
              IDX_N_HEADS * IDX_HEAD_DIM, IDX_HEAD_DIM, IDX_N_HEADS)
DSA_IN_DIM = sum(DSA_SPLITS)

ROPE_THETA = 500000.0
ROPE_DIM_ATT = ATT_HEAD_DIM // 4
ROPE_DIM_IDX = IDX_HEAD_DIM // 4
DEEPNORM_ALPHA = (2 * DEPTH) ** 0.25
DEEPNORM_BETA = (8 * DEPTH) ** -0.25
LN_EPS = 1e-5
RMS_EPS = 1e-5

kernel_name = "hybrid_ssd_dsa_deepnorm_trunk"


def layer_norm(x, g, b):
    xf = x.astype(jnp.float32)
    mu = jnp.mean(xf, axis=-1, keepdims=True)
    xc = xf - mu
    var = jnp.mean(xc * xc, axis=-1, keepdims=True)
    return (xc * lax.rsqrt(var + LN_EPS) * g.astype(jnp.float32) + b.astype(jnp.float32)).astype(x.dtype)


def partial_rope(x, pos, rot_dim):
    half = rot_dim // 2
    inv = jnp.power(ROPE_THETA, -2.0 * jnp.arange(half, dtype=jnp.float32) / rot_dim)
    ang = pos.astype(jnp.float32)[:, None] * inv[None, :]
    cos = jnp.cos(ang)[:, None, :]
    sin = jnp.sin(ang)[:, None, :]
    xf = x.astype(jnp.float32)
    x1 = xf[..., :half]
    x2 = xf[..., half:rot_dim]
    out = jnp.concatenate([x1 * cos - x2 * sin, x2 * cos + x1 * sin, xf[..., rot_dim:]], axis=-1)
    return out.astype(x.dtype)


def causal_dwconv(u, w, b):
    ch = u.shape[-1]
    out = lax.conv_general_dilated(
        u, w[:, None, :].astype(u.dtype), window_strides=(1,), padding=[(w.shape[0] - 1, 0)],
        dimension_numbers=("NWC", "WIO", "NWC"), feature_group_count=ch)
    return out + b.astype(u.dtype)


def ssd_scan(X, A, Bm, Cm):
    b, L, H, P = X.shape
    G, N = Bm.shape[2], Bm.shape[3]
    E = H // G
    Q = SSD_CHUNK
    c = L // Q
    X = X.reshape(b, c, Q, G, E, P)
    A = A.reshape(b, c, Q, G, E).transpose(0, 3, 4, 1, 2)
    Bm = Bm.reshape(b, c, Q, G, N)
    Cm = Cm.reshape(b, c, Q, G, N)
    A_cs = jnp.cumsum(A, axis=-1)
    causal = jnp.tril(jnp.ones((Q, Q), dtype=bool))
    seg = A_cs[..., :, None] - A_cs[..., None, :]
    CB = jnp.einsum('bcqgn,bcsgn->bgcqs', Cm, Bm)
    W = CB[:, :, None] * jnp.exp(jnp.where(causal, seg, -jnp.inf))
    y_diag = jnp.einsum('bgecqs,bcsgep->bcqgep', W, X)
    decay = jnp.exp(A_cs[..., -1:] - A_cs).transpose(0, 3, 4, 1, 2)
    states = jnp.einsum('bcqgn,bcqgep->cbgepn', Bm, X * decay[..., None])
    chunk_decay = jnp.exp(A_cs[..., -1]).transpose(3, 0, 1, 2)

    def step(h, inp):
        s, d = inp
        return d[..., None, None] * h + s, h

    h0 = jnp.zeros((b, G, E, P, N), dtype=X.dtype)
    _, prev = lax.scan(step, h0, (states, chunk_decay))
    in_decay = jnp.exp(A_cs).transpose(0, 3, 4, 1, 2)
    y_off = jnp.einsum('bcqgn,cbgepn->bcqgep', Cm, prev) * in_decay[..., None]
    return (y_diag + y_off).reshape(b, L, H, P)


def ssd_mixer(x, in_w, conv_w, conv_b, dt_bias, a_log, d_skip, norm_g, out_w):
    bsz, L, _ = x.shape
    proj = x @ in_w
    z = proj[..., :SSD_D_INNER]
    xbc = proj[..., SSD_D_INNER:SSD_D_INNER + SSD_CONV_DIM]
    dt_raw = proj[..., SSD_D_INNER + SSD_CONV_DIM:]
    xbc = jax.nn.silu(causal_dwconv(xbc, conv_w, conv_b)).astype(jnp.float32)
    gn = SSD_N_GROUPS * SSD_D_STATE
    xs = xbc[..., :SSD_D_INNER].reshape(bsz, L, SSD_N_HEADS, SSD_HEAD_DIM)
    Bm = xbc[..., SSD_D_INNER:SSD_D_INNER + gn].reshape(bsz, L, SSD_N_GROUPS, SSD_D_STATE)
    Cm = xbc[..., SSD_D_INNER + gn:].reshape(bsz, L, SSD_N_GROUPS, SSD_D_STATE)
    dt = jax.nn.softplus(dt_raw.astype(jnp.float32) + dt_bias.astype(jnp.float32))
    A = -jnp.exp(a_log.astype(jnp.float32))
    y = ssd_scan(xs * dt[..., None], dt * A, Bm, Cm)
    y = y + d_skip.astype(jnp.float32)[:, None] * xs
    y = y.reshape(bsz, L, SSD_D_INNER) * jax.nn.silu(z.astype(jnp.float32))
    yg = y.reshape(bsz, L, SSD_N_GROUPS, SSD_D_INNER // SSD_N_GROUPS)
    yg = yg * lax.rsqrt(jnp.mean(yg * yg, axis=-1, keepdims=True) + RMS_EPS)
    y = yg.reshape(bsz, L, SSD_D_INNER) * norm_g.astype(jnp.float32)
    return y.astype(x.dtype) @ out_w


def dsa_mixer(x, in_w, kn_g, kn_b, out_w):
    bsz, L, _ = x.shape
    proj = x @ in_w
    cuts = [int(c) for c in np.cumsum(DSA_SPLITS)[:-1]]
    q, k, v, z, qi, ki, wi = jnp.split(proj, cuts, axis=-1)
    pos = jnp.arange(L, dtype=jnp.int32)
    q = partial_rope(q.reshape(bsz, L, ATT_N_HEADS, ATT_HEAD_DIM), pos, ROPE_DIM_ATT)
    k = partial_rope(k.reshape(bsz, L, ATT_N_KV, ATT_HEAD_DIM), pos, ROPE_DIM_ATT)
    v = v.reshape(bsz, L, ATT_N_KV, ATT_HEAD_DIM)
    qi = partial_rope(qi.reshape(bsz, L, IDX_N_HEADS, IDX_HEAD_DIM), pos, ROPE_DIM_IDX)
    ki = partial_rope(layer_norm(ki, kn_g, kn_b)[:, :, None, :], pos, ROPE_DIM_IDX)[:, :, 0, :]
    ki = ki.astype(jnp.float32)
    wi = wi.astype(jnp.float32) * (IDX_N_HEADS ** -0.5 * IDX_HEAD_DIM ** -0.5)
    topk = min(TOPK_MAX, L // 4)
    key_chunk = pos // CHUNK
    scale = ATT_HEAD_DIM ** -0.5
    gather = jax.vmap(lambda arr, idx: arr[idx])
    nb = L // Q_BLOCK

    def to_blocks(a):
        return a.reshape((bsz, nb, Q_BLOCK) + a.shape[2:]).swapaxes(0, 1)

    def body(blk):
        qb, qib, wb, tb = blk
        q_chunk = tb // CHUNK
        rel = jax.nn.relu(jnp.einsum('bqhd,bsd->bqhs', qib.astype(jnp.float32), ki))
        score = jnp.einsum('bqhs,bqh->bqs', rel, wb)
        adm = key_chunk[None, :] <= q_chunk[:, None]
        score = jnp.where(adm[None], score, -jnp.inf)
        _, idx = lax.top_k(score, topk)
        valid = (idx // CHUNK) <= q_chunk[None, :, None]
        k_sel = gather(k, idx)
        v_sel = gather(v, idx)
        qg = qb.reshape(bsz, Q_BLOCK, ATT_N_KV, ATT_GROUP, ATT_HEAD_DIM)
        logits = jnp.einsum('bqhgd,bqkhd->bqhgk', qg, k_sel).astype(jnp.float32) * scale
        logits = jnp.where(valid[:, :, None, None, :], logits, -jnp.inf)
        p = jax.nn.softmax(logits, axis=-1).astype(v.dtype)
        o = jnp.einsum('bqhgk,bqkhd->bqhgd', p, v_sel)
        return o.reshape(bsz, Q_BLOCK, ATT_WIDTH)

    o = lax.map(body, (to_blocks(q), to_blocks(qi), to_blocks(wi), pos.reshape(nb, Q_BLOCK)))
    o = o.swapaxes(0, 1).reshape(bsz, L, ATT_WIDTH)
    o = o * jax.nn.silu(z)
    return o @ out_w


def setup_inputs(seed: int = 0) -> dict:
    key = jax.random.key(seed)
    ks = jax.random.split(key, 16)
    f32 = jnp.float32
    x = jax.random.normal(ks[0], (BATCH, SEQ, D_MODEL), f32)
    ssd_in_w = jax.random.normal(ks[1], (N_SSD_LAYERS, D_MODEL, SSD_IN_DIM), f32) * D_MODEL ** -0.5
    ssd_conv_w = jax.random.normal(ks[2], (N_SSD_LAYERS, SSD_CONV, SSD_CONV_DIM), f32) * SSD_CONV ** -0.5
    ssd_conv_b = 0.01 * jax.random.normal(ks[3], (N_SSD_LAYERS, SSD_CONV_DIM), f32)
    dt0 = jnp.exp(jax.random.uniform(ks[4], (N_SSD_LAYERS, SSD_N_HEADS), f32,
                                     math.log(1e-3), math.log(1e-1)))
    ssd_dt_bias = dt0 + jnp.log(-jnp.expm1(-dt0))
    ssd_a_log = jnp.log(jax.random.uniform(ks[5], (N_SSD_LAYERS, SSD_N_HEADS), f32, 1.0, 16.0))
    ssd_d_skip = 1.0 + 0.01 * jax.random.normal(ks[6], (N_SSD_LAYERS, SSD_N_HEADS), f32)
    ssd_norm_g = 1.0 + 0.01 * jax.random.normal(ks[7], (N_SSD_LAYERS, SSD_D_INNER), f32)
    ssd_out_w = jax.random.normal(ks[8], (N_SSD_LAYERS, SSD_D_INNER, D_MODEL), f32) * (SSD_D_INNER ** -0.5 * DEEPNORM_BETA)
    dsa_in_w = jax.random.normal(ks[9], (N_DSA_LAYERS, D_MODEL, DSA_IN_DIM), f32) * D_MODEL ** -0.5
    dsa_kn_g = 1.0 + 0.01 * jax.random.normal(ks[10], (N_DSA_LAYERS, IDX_HEAD_DIM), f32)
    dsa_kn_b = 0.01 * jax.random.normal(ks[11], (N_DSA_LAYERS, IDX_HEAD_DIM), f32)
    dsa_out_w = jax.random.normal(ks[12], (N_DSA_LAYERS, ATT_WIDTH, D_MODEL), f32) * (ATT_WIDTH ** -0.5 * DEEPNORM_BETA)
    ln_g = 1.0 + 0.01 * jax.random.normal(ks[13], (DEPTH, D_MODEL), f32)
    ln_b = 0.01 * jax.random.normal(ks[14], (DEPTH, D_MODEL), f32)
    return {"x": x, "ssd_in_w": ssd_in_w, "ssd_conv_w": ssd_conv_w, "ssd_conv_b": ssd_conv_b,
            "ssd_dt_bias": ssd_dt_bias, "ssd_a_log": ssd_a_log, "ssd_d_skip": ssd_d_skip,
            "ssd_norm_g": ssd_norm_g, "ssd_out_w": ssd_out_w, "dsa_in_w": dsa_in_w,
            "dsa_kn_g": dsa_kn_g, "dsa_kn_b": dsa_kn_b, "dsa_out_w": dsa_out_w,
            "ln_g": ln_g, "ln_b": ln_b}


def reference(x, ssd_in_w, ssd_conv_w, ssd_conv_b, ssd_dt_bias, ssd_a_log, ssd_d_skip,
              ssd_norm_g, ssd_out_w, dsa_in_w, dsa_kn_g, dsa_kn_b, dsa_out_w, ln_g, ln_b):
    for i in range(DEPTH):
        j = i // N_MIXERS
        if i % N_MIXERS == 0:
            f = ssd_mixer(x, ssd_in_w[j], ssd_conv_w[j], ssd_conv_b[j], ssd_dt_bias[j],
                          ssd_a_log[j], ssd_d_skip[j], ssd_norm_g[j], ssd_out_w[j])
        else:
            f = dsa_mixer(x, dsa_in_w[j], dsa_kn_g[j], dsa_kn_b[j], dsa_out_w[j])
        x = layer_norm(DEEPNORM_ALPHA * x + f, ln_g[i], ln_b[i])
    return x
```

```python
import functools
import math

import jax
import jax.numpy as jnp
from jax import lax
from jax.experimental import pallas as pl
from jax.experimental.pallas import tpu as pltpu

CHUNK = 64
CHUNK_SHIFT = 6
SSD_HEAD_SHIFT = 6
SSD_HEAD_DIM = 64
SSD_D_STATE = 128
SSD_N_GROUPS = 8
SSD_CONV = 4
SSD_CHUNK = 128
ATT_HEAD_DIM = 128
ATT_N_KV = 8
IDX_N_HEADS = 64
IDX_HEAD_DIM = 128
TOPK_MAX = 256
ROPE_THETA = 500000.0
ROPE_DIM = 32
LN_EPS = 1e-5
RMS_EPS = 1e-5

LANES = 128
SUBLANES = 8
V7X_VMEM_BYTES = 64 * 1024 * 1024
VMEM_LIMIT = V7X_VMEM_BYTES - 8 * 1024 * 1024

NEG_BIG = -1e30
INT_MIN = -2 ** 31

_HI = lax.Precision.HIGHEST


def _params(sem):
    return pltpu.CompilerParams(dimension_semantics=sem, vmem_limit_bytes=VMEM_LIMIT)


def _silu(v):
    return v * (1.0 / (1.0 + jnp.exp(-v)))


def _softplus(v):
    return jnp.maximum(v, 0.0) + jnp.log1p(jnp.exp(-jnp.abs(v)))


def _mm_kernel(a_ref, b_ref, o_ref):
    o_ref[...] = jnp.dot(a_ref[...], b_ref[...],
                         preferred_element_type=jnp.float32).astype(o_ref.dtype)


def _rope_heads(acc, cos, sa, sb, o_ref):
    for h in range(acc.shape[1] // LANES):
        a = acc[:, h * LANES:(h + 1) * LANES]
        r = (a * cos + pltpu.roll(a, LANES - ROPE_DIM // 2, 1) * sa
             + pltpu.roll(a, ROPE_DIM // 2, 1) * sb)
        o_ref[:, h * LANES:(h + 1) * LANES] = r.astype(o_ref.dtype)


def _mm_rope_kernel(a_ref, b_ref, cos_ref, sa_ref, sb_ref, o_ref, *, rope_ranges):
    acc = jnp.dot(a_ref[...], b_ref[...], preferred_element_type=jnp.float32)
    j = pl.program_id(1)
    is_rope = None
    for lo, hi in rope_ranges:
        t = jnp.logical_and(j >= lo, j < hi)
        is_rope = t if is_rope is None else jnp.logical_or(is_rope, t)

    @pl.when(is_rope)
    def _():
        _rope_heads(acc, cos_ref[...], sa_ref[...], sb_ref[...], o_ref)

    @pl.when(jnp.logical_not(is_rope))
    def _():
        o_ref[...] = acc.astype(o_ref.dtype)


def _matmul(a, b, out_dtype, tm, tn, rope=None):
    m, k = a.shape
    n = b.shape[1]
    tm, tn = min(tm, m), min(tn, n)
    assert m % tm == 0 and n % tn == 0
    in_specs = [pl.BlockSpec((tm, k), lambda i, j: (i, 0)),
                pl.BlockSpec((k, tn), lambda i, j: (0, j))]
    args = [a, b]
    if rope is None:
        body = _mm_kernel
    else:
        tables, ranges = rope
        body = functools.partial(_mm_rope_kernel, rope_ranges=ranges)
        in_specs += [pl.BlockSpec((tm, LANES), lambda i, j: (i, 0))] * 3
        args += list(tables)
    return pl.pallas_call(
        body, grid=(m // tm, n // tn), in_specs=in_specs,
        out_specs=pl.BlockSpec((tm, tn), lambda i, j: (i, j)),
        out_shape=jax.ShapeDtypeStruct((m, n), out_dtype),
        compiler_params=_params(("parallel", "parallel")),
        name="matmul_rope" if rope is not None else "matmul")(*args)


def _mm_ln_kernel(y_ref, w_ref, x_ref, g_ref, b_ref, o_ref, obf_ref, acc_ref, *, alpha, nk):
    k = pl.program_id(1)
    part = jnp.dot(y_ref[...], w_ref[...], preferred_element_type=jnp.float32)

    @pl.when(k == 0)
    def _():
        acc_ref[...] = part

    @pl.when(k > 0)
    def _():
        acc_ref[...] += part

    @pl.when(k == nk - 1)
    def _():
        h = alpha * x_ref[...] + acc_ref[...]
        mu = jnp.mean(h, axis=-1, keepdims=True)
        hc = h - mu
        var = jnp.mean(hc * hc, axis=-1, keepdims=True)
        out = hc * lax.rsqrt(var + LN_EPS) * g_ref[...] + b_ref[...]
        o_ref[...] = out
        obf_ref[...] = out.astype(obf_ref.dtype)


def _out_proj_ln(y, w, x, g, b, alpha, tm=256, tk=512):
    m, kdim = y.shape
    d = w.shape[1]
    tm, tk = min(tm, m), min(tk, kdim)
    nk = kdim // tk
    assert m % tm == 0 and kdim % tk == 0
    return pl.pallas_call(
        functools.partial(_mm_ln_kernel, alpha=alpha, nk=nk),
        grid=(m // tm, nk),
        in_specs=[pl.BlockSpec((tm, tk), lambda i, k: (i, k)),
                  pl.BlockSpec((tk, d), lambda i, k: (k, 0)),
                  pl.BlockSpec((tm, d), lambda i, k: (i, 0)),
                  pl.BlockSpec((1, d), lambda i, k: (0, 0)),
                  pl.BlockSpec((1, d), lambda i, k: (0, 0))],
        out_specs=[pl.BlockSpec((tm, d), lambda i, k: (i, 0)),
                   pl.BlockSpec((tm, d), lambda i, k: (i, 0))],
        out_shape=[jax.ShapeDtypeStruct((m, d), jnp.float32),
                   jax.ShapeDtypeStruct((m, d), jnp.bfloat16)],
        scratch_shapes=[pltpu.VMEM((tm, d), jnp.float32)],
        compiler_params=_params(("parallel", "arbitrary")),
        name="out_proj_ln")(y, w, x, g.reshape(1, d), b.reshape(1, d))


def _ssd_kernel(xr_ref, br_ref, cr_ref, z_ref, dt_ref, dtt_ref,
                cwx_ref, cwb_ref, cwc_ref, cbx_ref, cbb_ref, cbc_ref,
                bias_r_ref, alog_r_ref, bias_c_ref, alog_c_ref, dskip_ref, ng_ref,
                y_ref, xbuf, bbuf, cbuf, acsr_ref, h_ref, *, e_heads):
    g = pl.program_id(0)
    c = pl.program_id(1)
    q = SSD_CHUNK
    gw = xbuf.shape[1]
    hp = dt_ref.shape[1]
    tail = SUBLANES

    @pl.when(c == 0)
    def _():
        xbuf[0:tail, :] = jnp.zeros((tail, gw), jnp.float32)
        bbuf[0:tail, :] = jnp.zeros((tail, SSD_D_STATE), jnp.float32)
        cbuf[0:tail, :] = jnp.zeros((tail, SSD_D_STATE), jnp.float32)
        h_ref[...] = jnp.zeros_like(h_ref)

    def conv_silu(buf, raw_ref, w_ref, b_ref):
        buf[tail:tail + q, :] = raw_ref[...].astype(jnp.float32)
        w = w_ref[...]
        acc = b_ref[...] + w[0:1, :] * buf[tail - 3:tail - 3 + q, :]
        for j in range(1, SSD_CONV):
            acc = acc + w[j:j + 1, :] * buf[tail - 3 + j:tail - 3 + j + q, :]
        buf[0:tail, :] = buf[q:q + tail, :]
        return _silu(acc)

    xs = conv_silu(xbuf, xr_ref, cwx_ref, cbx_ref)
    bm = conv_silu(bbuf, br_ref, cwb_ref, cbb_ref)
    cm = conv_silu(cbuf, cr_ref, cwc_ref, cbc_ref)

    dt_c = _softplus(dt_ref[...] + bias_r_ref[...])
    da_c = dt_c * (-jnp.exp(alog_r_ref[...]))
    dt_r = _softplus(dtt_ref[...] + bias_c_ref[...])
    da_r = dt_r * (-jnp.exp(alog_c_ref[...]))
    ri = lax.broadcasted_iota(jnp.int32, (q, q), 0)
    ci = lax.broadcasted_iota(jnp.int32, (q, q), 1)
    causal = ri >= ci
    tril = causal.astype(jnp.float32)
    triu = (ri <= ci).astype(jnp.float32)
    acs_c = jnp.dot(tril, da_c, precision=_HI, preferred_element_type=jnp.float32)
    acsr_ref[...] = jnp.dot(da_r, triu, precision=_HI, preferred_element_type=jnp.float32)

    eh = lax.broadcasted_iota(jnp.int32, (hp, gw), 0)
    ec = lax.broadcasted_iota(jnp.int32, (hp, gw), 1)
    expand = (eh == g * e_heads + (ec >> SSD_HEAD_SHIFT)).astype(jnp.float32)
    dt_x = jnp.dot(dt_c, expand, precision=_HI, preferred_element_type=jnp.float32)
    acs_x = jnp.dot(acs_c, expand, precision=_HI, preferred_element_type=jnp.float32)
    last_x = acs_x[q - 1:q, :]
    x_dt = xs * dt_x
    x_dt_bf = x_dt.astype(jnp.bfloat16)
    x_dec_bf = (x_dt * jnp.exp(last_x - acs_x)).astype(jnp.bfloat16)

    bt = bm.T
    bt_bf = bt.astype(jnp.bfloat16)
    cm_bf = cm.astype(jnp.bfloat16)
    cb = jnp.dot(cm_bf, bt_bf, preferred_element_type=jnp.float32)

    h_prev = h_ref[...]
    y_off = jnp.dot(cm_bf, h_prev.astype(jnp.bfloat16), preferred_element_type=jnp.float32)
    y = y_off * jnp.exp(acs_x)
    s_new = jnp.dot(bt_bf, x_dec_bf, preferred_element_type=jnp.float32)
    h_ref[...] = jnp.exp(last_x) * h_prev + s_new

    lane = lax.broadcasted_iota(jnp.int32, (q, LANES), 1)
    lo = lane < SSD_HEAD_DIM
    y_parts = []
    for p in range(gw // LANES):
        col = acs_x[:, p * LANES:(p + 1) * LANES]
        col_sw = pltpu.roll(col, SSD_HEAD_DIM, 1)
        xp = x_dt_bf[:, p * LANES:(p + 1) * LANES]
        ws = []
        for half in range(2):
            colb = jnp.where(lo, col, col_sw) if half == 0 else jnp.where(lo, col_sw, col)
            hg = g * e_heads + 2 * p + half
            rowb = acsr_ref[pl.ds(hg, 1), :]
            seg = jnp.where(causal, colb - rowb, NEG_BIG)
            ws.append((cb * jnp.exp(seg)).astype(jnp.bfloat16))
        w2 = jnp.concatenate(ws, axis=1)
        zero = jnp.zeros_like(xp)
        x2 = jnp.concatenate([jnp.where(lo, xp, zero), jnp.where(lo, zero, xp)], axis=0)
        y_parts.append(jnp.dot(w2, x2, preferred_element_type=jnp.float32))
    y = y + jnp.concatenate(y_parts, axis=1)

    y = y + dskip_ref[...] * xs
    y = y * _silu(z_ref[...].astype(jnp.float32))
    ms = jnp.mean(y * y, axis=-1, keepdims=True)
    y_ref[...] = (y * lax.rsqrt(ms + RMS_EPS) * ng_ref[...]).astype(y_ref.dtype)


def _ssd_core(proj, dt_raw, dt_raw_t, conv_w, conv_b, dt_bias, a_log, d_skip, norm_g, d_inner):
    l = proj.shape[0]
    hp = dt_raw.shape[1]
    g_n = SSD_N_GROUPS
    gw = d_inner // g_n
    n = SSD_D_STATE
    e_heads = gw // SSD_HEAD_DIM
    n_heads = d_inner // SSD_HEAD_DIM
    q = SSD_CHUNK
    nc = l // q
    assert l % q == 0 and gw % LANES == 0

    def pad_h(v):
        return jnp.pad(v.astype(jnp.float32), (0, hp - n_heads))

    bias_p, alog_p = pad_h(dt_bias), pad_h(a_log)
    dskip_x = jnp.repeat(d_skip.astype(jnp.float32), SSD_HEAD_DIM).reshape(1, d_inner)
    xb = d_inner // gw
    bb = 2 * d_inner // n
    cbk = (2 * d_inner + g_n * n) // n
    cwb = d_inner // n
    cwc = (d_inner + g_n * n) // n
    conv_b2 = conv_b.reshape(1, -1)

    return pl.pallas_call(
        functools.partial(_ssd_kernel, e_heads=e_heads),
        grid=(g_n, nc),
        in_specs=[
            pl.BlockSpec((q, gw), lambda g, c: (c, xb + g)),
            pl.BlockSpec((q, n), lambda g, c: (c, bb + g)),
            pl.BlockSpec((q, n), lambda g, c: (c, cbk + g)),
            pl.BlockSpec((q, gw), lambda g, c: (c, g)),
            pl.BlockSpec((q, hp), lambda g, c: (c, 0)),
            pl.BlockSpec((hp, q), lambda g, c: (0, c)),
            pl.BlockSpec((SSD_CONV, gw), lambda g, c: (0, g)),
            pl.BlockSpec((SSD_CONV, n), lambda g, c: (0, cwb + g)),
            pl.BlockSpec((SSD_CONV, n), lambda g, c: (0, cwc + g)),
            pl.BlockSpec((1, gw), lambda g, c: (0, g)),
            pl.BlockSpec((1, n), lambda g, c: (0, cwb + g)),
            pl.BlockSpec((1, n), lambda g, c: (0, cwc + g)),
            pl.BlockSpec((1, hp), lambda g, c: (0, 0)),
            pl.BlockSpec((1, hp), lambda g, c: (0, 0)),
            pl.BlockSpec((hp, 1), lambda g, c: (0, 0)),
            pl.BlockSpec((hp, 1), lambda g, c: (0, 0)),
            pl.BlockSpec((1, gw), lambda g, c: (0, g)),
            pl.BlockSpec((1, gw), lambda g, c: (0, g)),
        ],
        out_specs=pl.BlockSpec((q, gw), lambda g, c: (c, g)),
        out_shape=jax.ShapeDtypeStruct((l, d_inner), jnp.bfloat16),
        scratch_shapes=[pltpu.VMEM((q + SUBLANES, gw), jnp.float32),
                        pltpu.VMEM((q + SUBLANES, n), jnp.float32),
                        pltpu.VMEM((q + SUBLANES, n), jnp.float32),
                        pltpu.VMEM((hp, q), jnp.float32),
                        pltpu.VMEM((n, gw), jnp.float32)],
        compiler_params=_params(("parallel", "arbitrary")),
        name="ssd_core")(
            proj, proj, proj, proj, dt_raw, dt_raw_t,
            conv_w, conv_w, conv_w, conv_b2, conv_b2, conv_b2,
            bias_p.reshape(1, hp), alog_p.reshape(1, hp), bias_p.reshape(hp, 1), alog_p.reshape(hp, 1),
            dskip_x, norm_g.astype(jnp.float32).reshape(1, d_inner))


def _idx_small_kernel(a_ref, w_ref, g_ref, b_ref, cos_ref, sa_ref, sb_ref, ki_ref, wi_ref, *, w_scale):
    acc = jnp.dot(a_ref[...], w_ref[...], preferred_element_type=jnp.float32)
    ki = acc[:, :IDX_HEAD_DIM]
    mu = jnp.mean(ki, axis=-1, keepdims=True)
    kc = ki - mu
    var = jnp.mean(kc * kc, axis=-1, keepdims=True)
    kn = kc * lax.rsqrt(var + LN_EPS) * g_ref[...] + b_ref[...]
    _rope_heads(kn, cos_ref[...], sa_ref[...], sb_ref[...], ki_ref)
    wi_ref[...] = acc[:, IDX_HEAD_DIM:] * w_scale


def _idx_small(xb, w_small, kn_g, kn_b, tables, tm=512):
    m, k = xb.shape
    tm = min(tm, m)
    w_scale = IDX_N_HEADS ** -0.5 * IDX_HEAD_DIM ** -0.5
    return pl.pallas_call(
        functools.partial(_idx_small_kernel, w_scale=w_scale),
        grid=(m // tm,),
        in_specs=[pl.BlockSpec((tm, k), lambda i: (i, 0)),
                  pl.BlockSpec((k, 2 * LANES), lambda i: (0, 0)),
                  pl.BlockSpec((1, LANES), lambda i: (0, 0)),
                  pl.BlockSpec((1, LANES), lambda i: (0, 0))]
                 + [pl.BlockSpec((tm, LANES), lambda i: (i, 0))] * 3,
        out_specs=[pl.BlockSpec((tm, LANES), lambda i: (i, 0)),
                   pl.BlockSpec((tm, LANES), lambda i: (i, 0))],
        out_shape=[jax.ShapeDtypeStruct((m, LANES), jnp.bfloat16),
                   jax.ShapeDtypeStruct((m, LANES), jnp.float32)],
        compiler_params=_params(("parallel",)),
        name="idx_small")(xb, w_small, kn_g.reshape(1, -1).astype(jnp.float32),
                          kn_b.reshape(1, -1).astype(jnp.float32), *tables)


def _index_select_kernel(qi_ref, wi_ref, kit_ref, bias_ref, keys_ref, *, tq, tk, topk):
    i = pl.program_id(0)
    l = kit_ref.shape[1]
    n_kt = ((i + 1) * tq + tk - 1) // tk
    q_chunk = (i * tq + lax.broadcasted_iota(jnp.int32, (tq, tk), 0)) >> CHUNK_SHIFT
    wi = wi_ref[...]

    def score_tile(jt, carry):
        k0 = pl.multiple_of(jt * tk, tk)
        kt = kit_ref[:, pl.ds(k0, tk)]
        s = jnp.zeros((tq, tk), jnp.float32)
        for h in range(IDX_N_HEADS):
            d = jnp.dot(qi_ref[:, h * LANES:(h + 1) * LANES], kt, preferred_element_type=jnp.float32)
            s = s + wi[:, h:h + 1] * jnp.maximum(d, 0.0)
        k_chunk = (k0 + lax.broadcasted_iota(jnp.int32, (tq, tk), 1)) >> CHUNK_SHIFT
        bits = lax.bitcast_convert_type(s + 0.0, jnp.int32)
        key = jnp.where(bits < 0, bits ^ jnp.int32(0x7FFFFFFF), bits)
        keys_ref[:, pl.ds(k0, tk)] = jnp.where(k_chunk <= q_chunk, key, jnp.int32(INT_MIN))
        return carry

    lax.fori_loop(0, n_kt, score_tile, 0)

    def bit_step(it, t):
        cand = t ^ lax.shift_left(jnp.int32(1), jnp.int32(31) - it)

        def count_tile(jt, cnt):
            k0 = pl.multiple_of(jt * tk, tk)
            blk = keys_ref[:, pl.ds(k0, tk)]
            ge = (blk >= cand).astype(jnp.int32)
            for cblk in range(tk // LANES):
                cnt = cnt + ge[:, cblk * LANES:(cblk + 1) * LANES]
            return cnt

        cnt = lax.fori_loop(0, n_kt, count_tile, jnp.zeros((tq, LANES), jnp.int32))
        total = jnp.sum(cnt.astype(jnp.float32), axis=1, keepdims=True)
        return jnp.where(total >= float(topk), cand, t)

    t = lax.fori_loop(0, 32, bit_step, jnp.full((tq, 1), INT_MIN, jnp.int32))
    t = jnp.maximum(t, jnp.int32(INT_MIN + 1))

    def write_tile(jt, carry):
        k0 = pl.multiple_of(jt * tk, tk)
        sel = keys_ref[:, pl.ds(k0, tk)] >= t
        bias_ref[:, pl.ds(k0, tk)] = jnp.where(sel, 0.0, NEG_BIG).astype(bias_ref.dtype)
        return carry

    lax.fori_loop(0, n_kt, write_tile, 0)

    def fill_tile(jt, carry):
        k0 = pl.multiple_of(jt * tk, tk)
        bias_ref[:, pl.ds(k0, tk)] = jnp.full((tq, tk), NEG_BIG, bias_ref.dtype)
        return carry

    lax.fori_loop(n_kt, l // tk, fill_tile, 0)


def _index_select(proj, wi, ki_t, topk, tq=128, tk=256):
    l = proj.shape[0]
    tq, tk = min(tq, l), min(tk, l)
    qw = IDX_N_HEADS * IDX_HEAD_DIM
    return pl.pallas_call(
        functools.partial(_index_select_kernel, tq=tq, tk=tk, topk=topk),
        grid=(l // tq,),
        in_specs=[pl.BlockSpec((tq, qw), lambda i: (i, 0)),
                  pl.BlockSpec((tq, LANES), lambda i: (i, 0)),
                  pl.BlockSpec((IDX_HEAD_DIM, l), lambda i: (0, 0))],
        out_specs=pl.BlockSpec((tq, l), lambda i: (i, 0)),
        out_shape=jax.ShapeDtypeStruct((l, l), jnp.bfloat16),
        scratch_shapes=[pltpu.VMEM((tq, l), jnp.int32)],
        compiler_params=_params(("parallel",)),
        name="index_select")(proj, wi, ki_t)


def _attn_kernel(q_ref, z_ref, kt_ref, v_ref, bias_ref, o_ref, m_ref, l_ref, acc_ref, *, tq, tk, group):
    i = pl.program_id(0)
    j = pl.program_id(1)
    n_kt = ((i + 1) * tq + tk - 1) // tk
    scale = ATT_HEAD_DIM ** -0.5

    @pl.when(j == 0)
    def _():
        m_ref[...] = jnp.full_like(m_ref, NEG_BIG)
        l_ref[...] = jnp.zeros_like(l_ref)
        acc_ref[...] = jnp.zeros_like(acc_ref)

    @pl.when(j < n_kt)
    def _():
        bias = bias_ref[...].astype(jnp.float32)
        for kv in range(ATT_N_KV):
            kt = kt_ref[kv * LANES:(kv + 1) * LANES, :]
            vv = v_ref[:, kv * LANES:(kv + 1) * LANES]
            for gi in range(group):
                h = kv * group + gi
                s = jnp.dot(q_ref[:, h * LANES:(h + 1) * LANES], kt,
                            preferred_element_type=jnp.float32) * scale + bias
                m_old = m_ref[h]
                m_new = jnp.maximum(m_old, jnp.max(s, axis=1, keepdims=True))
                p = jnp.exp(s - m_new[:, :1])
                alpha = jnp.exp(m_old - m_new)
                l_ref[h] = alpha * l_ref[h] + jnp.sum(p, axis=1, keepdims=True)
                m_ref[h] = m_new
                acc_ref[:, h * LANES:(h + 1) * LANES] = (
                    alpha * acc_ref[:, h * LANES:(h + 1) * LANES]
                    + jnp.dot(p.astype(jnp.bfloat16), vv, preferred_element_type=jnp.float32))

    @pl.when(j == n_kt - 1)
    def _():
        zg = _silu(z_ref[...].astype(jnp.float32))
        for h in range(ATT_N_KV * group):
            o = acc_ref[:, h * LANES:(h + 1) * LANES] / l_ref[h]
            o_ref[:, h * LANES:(h + 1) * LANES] = (
                o * zg[:, h * LANES:(h + 1) * LANES]).astype(o_ref.dtype)


def _attention(proj, k_t, bias, q_blk, z_blk, v_blk, d_att, tq=256, tk=512):
    l = proj.shape[0]
    tq, tk = min(tq, l), min(tk, l)
    n_heads = d_att // ATT_HEAD_DIM
    group = n_heads // ATT_N_KV
    kvw = ATT_N_KV * ATT_HEAD_DIM

    def last_tile(i):
        return ((i + 1) * tq + tk - 1) // tk - 1

    return pl.pallas_call(
        functools.partial(_attn_kernel, tq=tq, tk=tk, group=group),
        grid=(l // tq, l // tk),
        in_specs=[pl.BlockSpec((tq, d_att), lambda i, j: (i, q_blk)),
                  pl.BlockSpec((tq, d_att), lambda i, j: (i, z_blk)),
                  pl.BlockSpec((kvw, tk), lambda i, j: (0, jnp.minimum(j, last_tile(i)))),
                  pl.BlockSpec((tk, kvw), lambda i, j: (jnp.minimum(j, last_tile(i)), v_blk)),
                  pl.BlockSpec((tq, tk), lambda i, j: (i, jnp.minimum(j, last_tile(i))))],
        out_specs=pl.BlockSpec((tq, d_att), lambda i, j: (i, 0)),
        out_shape=jax.ShapeDtypeStruct((l, d_att), jnp.bfloat16),
        scratch_shapes=[pltpu.VMEM((n_heads, tq, LANES), jnp.float32),
                        pltpu.VMEM((n_heads, tq, LANES), jnp.float32),
                        pltpu.VMEM((tq, d_att), jnp.float32)],
        compiler_params=_params(("parallel", "arbitrary")),
        name="sparse_attention")(proj, proj, k_t, proj, bias)


def _rope_tables(l):
    half = ROPE_DIM // 2
    inv = jnp.power(ROPE_THETA, -2.0 * jnp.arange(half, dtype=jnp.float32) / ROPE_DIM)
    ang = jnp.arange(l, dtype=jnp.float32)[:, None] * inv[None, :]
    cos, sin = jnp.cos(ang), jnp.sin(ang)
    zeros = jnp.zeros((l, LANES - ROPE_DIM), jnp.float32)
    zh = jnp.zeros((l, half), jnp.float32)
    cos_t = jnp.concatenate([cos, cos, jnp.ones_like(zeros)], axis=1)
    sa_t = jnp.concatenate([-sin, zh, zeros], axis=1)
    sb_t = jnp.concatenate([zh, sin, zeros], axis=1)
    return cos_t, sa_t, sb_t


def _ssd_layer(x_f32, x_bf, in_w, conv_w, conv_b, dt_bias, a_log, d_skip, norm_g, out_w, ln_g, ln_b, alpha):
    d = x_bf.shape[1]
    d_inner = out_w.shape[0]
    n_heads = d_inner // SSD_HEAD_DIM
    main_w = 2 * d_inner + 2 * SSD_N_GROUPS * SSD_D_STATE
    hp = -(-n_heads // LANES) * LANES
    w_main = in_w[:, :main_w].astype(jnp.bfloat16)
    w_dt = jnp.pad(in_w[:, main_w:], ((0, 0), (0, hp - n_heads))).astype(jnp.bfloat16)
    proj = _matmul(x_bf, w_main, jnp.bfloat16, 1024, 1024)
    dt_raw = _matmul(x_bf, w_dt, jnp.float32, 1024, LANES)
    y = _ssd_core(proj, dt_raw, dt_raw.T, conv_w, conv_b, dt_bias, a_log, d_skip, norm_g, d_inner)
    return _out_proj_ln(y, out_w.astype(jnp.bfloat16), x_f32, ln_g, ln_b, alpha)


def _dsa_layer(x_f32, x_bf, in_w, kn_g, kn_b, out_w, ln_g, ln_b, alpha, tables):
    l, d = x_bf.shape
    d_att = out_w.shape[0]
    kvw = ATT_N_KV * ATT_HEAD_DIM
    qw = IDX_N_HEADS * IDX_HEAD_DIM
    o_q, o_k, o_v, o_z = 0, d_att, d_att + kvw, d_att + 2 * kvw
    o_qi = o_z + d_att
    o_ki = o_qi + qw
    o_wi = o_ki + IDX_HEAD_DIM
    tn = 1024
    assert d_att % tn == 0 and qw % d_att == 0
    w_main = jnp.concatenate([in_w[:, o_qi:o_ki], in_w[:, o_q:o_k], in_w[:, o_z:o_qi],
                              in_w[:, o_k:o_v], in_w[:, o_v:o_z]], axis=1).astype(jnp.bfloat16)
    w_small = jnp.pad(in_w[:, o_ki:], ((0, 0), (0, 2 * LANES - IDX_HEAD_DIM - IDX_N_HEADS))).astype(jnp.bfloat16)
    c_q, c_z, c_k, c_v = qw, qw + d_att, qw + 2 * d_att, qw + 2 * d_att + kvw
    rope_ranges = ((0, (c_q + d_att) // tn), (c_k // tn, c_v // tn))
    proj = _matmul(x_bf, w_main, jnp.bfloat16, 1024, tn, rope=(tables, rope_ranges))
    ki, wi = _idx_small(x_bf, w_small, kn_g, kn_b, tables)
    topk = min(TOPK_MAX, l // 4)
    bias = _index_select(proj, wi, ki.T, topk)
    k_t = proj[:, c_k:c_v].T
    o = _attention(proj, k_t, bias, c_q // d_att, c_z // d_att, c_v // kvw, d_att)
    return _out_proj_ln(o, out_w.astype(jnp.bfloat16), x_f32, ln_g, ln_b, alpha)


def kernel(x, ssd_in_w, ssd_conv_w, ssd_conv_b, ssd_dt_bias, ssd_a_log, ssd_d_skip, ssd_norm_g, ssd_out_w,
           dsa_in_w, dsa_kn_g, dsa_kn_b, dsa_out_w, ln_g, ln_b):
    bsz, l, d = x.shape
    depth = ln_g.shape[0]
    alpha = (2 * depth) ** 0.25
    tables = _rope_tables(l)
    outs = []
    for bi in range(bsz):
        xf = x[bi].astype(jnp.float32)
        xb = xf.astype(jnp.bfloat16)
        for i in range(depth):
            j = i // 2
            if i % 2 == 0:
                xf, xb = _ssd_layer(xf, xb, ssd_in_w[j], ssd_conv_w[j], ssd_conv_b[j], ssd_dt_bias[j],
                                    ssd_a_log[j], ssd_d_skip[j], ssd_norm_g[j], ssd_out_w[j],
                                    ln_g[i], ln_b[i], alpha)
            else:
                xf, xb = _dsa_layer(xf, xb, dsa_in_w[j], dsa_kn_g[j], dsa_kn_b[j], dsa_out_w[j],
                                    ln_g[i], ln_b[i], alpha, tables)
        outs.append(xf)
    return jnp.stack(outs).astype(x.dtype)
```

```python
import functools
import math

import jax
import jax.numpy as jnp
from jax import lax
from jax.experimental import pallas as pl
from jax.experimental.pallas import tpu as pltpu

CHUNK = 64
CHUNK_SHIFT = 6
SSD_HEAD_SHIFT = 6
SSD_HEAD_DIM = 64
SSD_D_STATE = 128
SSD_N_GROUPS = 8
SSD_CONV = 4
SSD_CHUNK = 128
ATT_HEAD_DIM = 128
ATT_N_KV = 8
IDX_N_HEADS = 64
IDX_HEAD_DIM = 128
TOPK_MAX = 256
ROPE_THETA = 500000.0
ROPE_DIM = 32
LN_EPS = 1e-5
RMS_EPS = 1e-5

LANES = 128
SUBLANES = 8
V7X_VMEM_BYTES = 64 * 1024 * 1024
VMEM_LIMIT = V7X_VMEM_BYTES - 8 * 1024 * 1024

NEG_BIG = -1e30
INT_MIN = -2 ** 31

_HI = lax.Precision.HIGHEST


def _params(sem):
    return pltpu.CompilerParams(dimension_semantics=sem, vmem_limit_bytes=VMEM_LIMIT)


def _silu(v):
    return v * (1.0 / (1.0 + jnp.exp(-v)))


def _softplus(v):
    return jnp.maximum(v, 0.0) + jnp.log1p(jnp.exp(-jnp.abs(v)))


def _mm_kernel(a_ref, b_ref, o_ref):
    o_ref[...] = jnp.dot(a_ref[...], b_ref[...],
                         preferred_element_type=jnp.float32).astype(o_ref.dtype)


def _rope_heads(acc, cos, sa, sb, o_ref):
    for h in range(acc.shape[1] // LANES):
        a = acc[:, h * LANES:(h + 1) * LANES]
        r = (a * cos + pltpu.roll(a, LANES - ROPE_DIM // 2, 1) * sa
             + pltpu.roll(a, ROPE_DIM // 2, 1) * sb)
        o_ref[:, h * LANES:(h + 1) * LANES] = r.astype(o_ref.dtype)


def _mm_rope_kernel(a_ref, b_ref, cos_ref, sa_ref, sb_ref, o_ref, *, rope_ranges, scaled_range, scale):
    acc = jnp.dot(a_ref[...], b_ref[...], preferred_element_type=jnp.float32)
    j = pl.program_id(1)
    is_rope = None
    for lo, hi in rope_ranges:
        t = jnp.logical_and(j >= lo, j < hi)
        is_rope = t if is_rope is None else jnp.logical_or(is_rope, t)

    @pl.when(is_rope)
    def _():
        in_scaled = jnp.logical_and(j >= scaled_range[0], j < scaled_range[1])
        f = jnp.where(in_scaled, jnp.float32(scale), jnp.float32(1.0))
        _rope_heads(acc * f, cos_ref[...], sa_ref[...], sb_ref[...], o_ref)

    @pl.when(jnp.logical_not(is_rope))
    def _():
        o_ref[...] = acc.astype(o_ref.dtype)


def _matmul(a, b, out_dtype, tm, tn, rope=None):
    m, k = a.shape
    n = b.shape[1]
    tm, tn = min(tm, m), min(tn, n)
    assert m % tm == 0 and n % tn == 0
    in_specs = [pl.BlockSpec((tm, k), lambda i, j: (i, 0)),
                pl.BlockSpec((k, tn), lambda i, j: (0, j))]
    args = [a, b]
    if rope is None:
        body = _mm_kernel
    else:
        tables, ranges, scaled_range, scale = rope
        body = functools.partial(_mm_rope_kernel, rope_ranges=ranges, scaled_range=scaled_range, scale=scale)
        in_specs += [pl.BlockSpec((tm, LANES), lambda i, j: (i, 0))] * 3
        args += list(tables)
    return pl.pallas_call(
        body, grid=(m // tm, n // tn), in_specs=in_specs,
        out_specs=pl.BlockSpec((tm, tn), lambda i, j: (i, j)),
        out_shape=jax.ShapeDtypeStruct((m, n), out_dtype),
        compiler_params=_params(("parallel", "parallel")),
        name="matmul_rope" if rope is not None else "matmul")(*args)


LN_ROWS = 64


def _mm_ln_kernel(y_ref, w_ref, x_ref, g_ref, b_ref, o_ref, obf_ref, *, alpha, nk, n_split):
    k = pl.program_id(1)
    tm, d = o_ref.shape
    dn = d // n_split

    @pl.when(k == 0)
    def _():
        o_ref[...] = alpha * x_ref[...]

    for c in range(n_split):
        cols = slice(c * dn, (c + 1) * dn)
        o_ref[:, cols] += jnp.dot(y_ref[...], w_ref[:, cols], preferred_element_type=jnp.float32)

    @pl.when(k == nk - 1)
    def _():
        def ln_rows(r, carry):
            rows = pl.ds(pl.multiple_of(r * LN_ROWS, LN_ROWS), LN_ROWS)
            h = o_ref[rows, :]
            mu = jnp.mean(h, axis=-1, keepdims=True)
            hc = h - mu
            var = jnp.mean(hc * hc, axis=-1, keepdims=True)
            out = hc * lax.rsqrt(var + LN_EPS) * g_ref[...] + b_ref[...]
            o_ref[rows, :] = out
            obf_ref[rows, :] = out.astype(obf_ref.dtype)
            return carry

        lax.fori_loop(0, tm // LN_ROWS, ln_rows, 0)


def _out_proj_ln(y, w, x, g, b, alpha, tm=512, tk=512):
    m, kdim = y.shape
    d = w.shape[1]
    tm, tk = min(tm, m), min(tk, kdim)
    nk = kdim // tk
    n_split = 2 if d % (2 * LANES) == 0 else 1
    assert m % tm == 0 and kdim % tk == 0 and tm % LN_ROWS == 0
    return pl.pallas_call(
        functools.partial(_mm_ln_kernel, alpha=alpha, nk=nk, n_split=n_split),
        grid=(m // tm, nk),
        in_specs=[pl.BlockSpec((tm, tk), lambda i, k: (i, k)),
                  pl.BlockSpec((tk, d), lambda i, k: (k, 0)),
                  pl.BlockSpec((tm, d), lambda i, k: (i, 0)),
                  pl.BlockSpec((1, d), lambda i, k: (0, 0)),
                  pl.BlockSpec((1, d), lambda i, k: (0, 0))],
        out_specs=[pl.BlockSpec((tm, d), lambda i, k: (i, 0)),
                   pl.BlockSpec((tm, d), lambda i, k: (i, 0))],
        out_shape=[jax.ShapeDtypeStruct((m, d), jnp.float32),
                   jax.ShapeDtypeStruct((m, d), jnp.bfloat16)],
        compiler_params=_params(("parallel", "arbitrary")),
        name="out_proj_ln")(y, w, x, g.reshape(1, d), b.reshape(1, d))


def _ssd_kernel(xr_ref, br_ref, cr_ref, z_ref, dt_ref, dtt_ref,
                cwx_ref, cwb_ref, cwc_ref, cbx_ref, cbb_ref, cbc_ref,
                bias_r_ref, alog_r_ref, bias_c_ref, alog_c_ref, dskip_ref, ng_ref,
                y_ref, xbuf, bbuf, cbuf, acsr_ref, h_ref, *, e_heads):
    g = pl.program_id(0)
    c = pl.program_id(1)
    q = SSD_CHUNK
    gw = xbuf.shape[1]
    hp = dt_ref.shape[1]
    tail = SUBLANES

    @pl.when(c == 0)
    def _():
        xbuf[0:tail, :] = jnp.zeros((tail, gw), jnp.float32)
        bbuf[0:tail, :] = jnp.zeros((tail, SSD_D_STATE), jnp.float32)
        cbuf[0:tail, :] = jnp.zeros((tail, SSD_D_STATE), jnp.float32)
        h_ref[...] = jnp.zeros_like(h_ref)

    def conv_silu(buf, raw_ref, w_ref, b_ref):
        buf[tail:tail + q, :] = raw_ref[...].astype(jnp.float32)
        w = w_ref[...]
        acc = b_ref[...] + w[0:1, :] * buf[tail - 3:tail - 3 + q, :]
        for j in range(1, SSD_CONV):
            acc = acc + w[j:j + 1, :] * buf[tail - 3 + j:tail - 3 + j + q, :]
        buf[0:tail, :] = buf[q:q + tail, :]
        return _silu(acc)

    xs = conv_silu(xbuf, xr_ref, cwx_ref, cbx_ref)
    bm = conv_silu(bbuf, br_ref, cwb_ref, cbb_ref)
    cm = conv_silu(cbuf, cr_ref, cwc_ref, cbc_ref)

    dt_c = _softplus(dt_ref[...] + bias_r_ref[...])
    da_c = dt_c * (-jnp.exp(alog_r_ref[...]))
    dt_r = _softplus(dtt_ref[...] + bias_c_ref[...])
    da_r = dt_r * (-jnp.exp(alog_c_ref[...]))
    ri = lax.broadcasted_iota(jnp.int32, (q, q), 0)
    ci = lax.broadcasted_iota(jnp.int32, (q, q), 1)
    causal = ri >= ci
    tril = causal.astype(jnp.float32)
    triu = (ri <= ci).astype(jnp.float32)
    acs_c = jnp.dot(tril, da_c, precision=_HI, preferred_element_type=jnp.float32)
    acsr_ref[...] = jnp.dot(da_r, triu, precision=_HI, preferred_element_type=jnp.float32)

    eh = lax.broadcasted_iota(jnp.int32, (hp, gw), 0)
    ec = lax.broadcasted_iota(jnp.int32, (hp, gw), 1)
    expand = (eh == g * e_heads + (ec >> SSD_HEAD_SHIFT)).astype(jnp.float32)
    dt_x = jnp.dot(dt_c, expand, precision=_HI, preferred_element_type=jnp.float32)
    acs_x = jnp.dot(acs_c, expand, precision=_HI, preferred_element_type=jnp.float32)
    last_x = acs_x[q - 1:q, :]
    x_dt = xs * dt_x
    x_dt_bf = x_dt.astype(jnp.bfloat16)
    x_dec_bf = (x_dt * jnp.exp(last_x - acs_x)).astype(jnp.bfloat16)

    bt = bm.T
    bt_bf = bt.astype(jnp.bfloat16)
    cm_bf = cm.astype(jnp.bfloat16)
    cb = jnp.dot(cm_bf, bt_bf, preferred_element_type=jnp.float32)

    h_prev = h_ref[...]
    y_off = jnp.dot(cm_bf, h_prev.astype(jnp.bfloat16), preferred_element_type=jnp.float32)
    y = y_off * jnp.exp(acs_x)
    s_new = jnp.dot(bt_bf, x_dec_bf, preferred_element_type=jnp.float32)
    h_ref[...] = jnp.exp(last_x) * h_prev + s_new

    lane = lax.broadcasted_iota(jnp.int32, (q, LANES), 1)
    lo = lane < SSD_HEAD_DIM
    y_parts = []
    for p in range(gw // LANES):
        col = acs_x[:, p * LANES:(p + 1) * LANES]
        col_sw = pltpu.roll(col, SSD_HEAD_DIM, 1)
        xp = x_dt_bf[:, p * LANES:(p + 1) * LANES]
        ws = []
        for half in range(2):
            colb = jnp.where(lo, col, col_sw) if half == 0 else jnp.where(lo, col_sw, col)
            hg = g * e_heads + 2 * p + half
            rowb = acsr_ref[pl.ds(hg, 1), :]
            seg = jnp.where(causal, colb - rowb, NEG_BIG)
            ws.append((cb * jnp.exp(seg)).astype(jnp.bfloat16))
        w2 = jnp.concatenate(ws, axis=1)
        zero = jnp.zeros_like(xp)
        x2 = jnp.concatenate([jnp.where(lo, xp, zero), jnp.where(lo, zero, xp)], axis=0)
        y_parts.append(jnp.dot(w2, x2, preferred_element_type=jnp.float32))
    y = y + jnp.concatenate(y_parts, axis=1)

    y = y + dskip_ref[...] * xs
    y = y * _silu(z_ref[...].astype(jnp.float32))
    ms = jnp.mean(y * y, axis=-1, keepdims=True)
    y_ref[...] = (y * lax.rsqrt(ms + RMS_EPS) * ng_ref[...]).astype(y_ref.dtype)


def _ssd_core(proj, dt_raw, dt_raw_t, conv_w, conv_b, dt_bias, a_log, d_skip, norm_g, d_inner):
    l = proj.shape[0]
    hp = dt_raw.shape[1]
    g_n = SSD_N_GROUPS
    gw = d_inner // g_n
    n = SSD_D_STATE
    e_heads = gw // SSD_HEAD_DIM
    n_heads = d_inner // SSD_HEAD_DIM
    q = SSD_CHUNK
    nc = l // q
    assert l % q == 0 and gw % LANES == 0

    def pad_h(v):
        return jnp.pad(v.astype(jnp.float32), (0, hp - n_heads))

    bias_p, alog_p = pad_h(dt_bias), pad_h(a_log)
    dskip_x = jnp.repeat(d_skip.astype(jnp.float32), SSD_HEAD_DIM).reshape(1, d_inner)
    xb = d_inner // gw
    bb = 2 * d_inner // n
    cbk = (2 * d_inner + g_n * n) // n
    cwb = d_inner // n
    cwc = (d_inner + g_n * n) // n
    conv_b2 = conv_b.reshape(1, -1)

    return pl.pallas_call(
        functools.partial(_ssd_kernel, e_heads=e_heads),
        grid=(g_n, nc),
        in_specs=[
            pl.BlockSpec((q, gw), lambda g, c: (c, xb + g)),
            pl.BlockSpec((q, n), lambda g, c: (c, bb + g)),
            pl.BlockSpec((q, n), lambda g, c: (c, cbk + g)),
            pl.BlockSpec((q, gw), lambda g, c: (c, g)),
            pl.BlockSpec((q, hp), lambda g, c: (c, 0)),
            pl.BlockSpec((hp, q), lambda g, c: (0, c)),
            pl.BlockSpec((SSD_CONV, gw), lambda g, c: (0, g)),
            pl.BlockSpec((SSD_CONV, n), lambda g, c: (0, cwb + g)),
            pl.BlockSpec((SSD_CONV, n), lambda g, c: (0, cwc + g)),
            pl.BlockSpec((1, gw), lambda g, c: (0, g)),
            pl.BlockSpec((1, n), lambda g, c: (0, cwb + g)),
            pl.BlockSpec((1, n), lambda g, c: (0, cwc + g)),
            pl.BlockSpec((1, hp), lambda g, c: (0, 0)),
            pl.BlockSpec((1, hp), lambda g, c: (0, 0)),
            pl.BlockSpec((hp, 1), lambda g, c: (0, 0)),
            pl.BlockSpec((hp, 1), lambda g, c: (0, 0)),
            pl.BlockSpec((1, gw), lambda g, c: (0, g)),
            pl.BlockSpec((1, gw), lambda g, c: (0, g)),
        ],
        out_specs=pl.BlockSpec((q, gw), lambda g, c: (c, g)),
        out_shape=jax.ShapeDtypeStruct((l, d_inner), jnp.bfloat16),
        scratch_shapes=[pltpu.VMEM((q + SUBLANES, gw), jnp.float32),
                        pltpu.VMEM((q + SUBLANES, n), jnp.float32),
                        pltpu.VMEM((q + SUBLANES, n), jnp.float32),
                        pltpu.VMEM((hp, q), jnp.float32),
                        pltpu.VMEM((n, gw), jnp.float32)],
        compiler_params=_params(("parallel", "arbitrary")),
        name="ssd_core")(
            proj, proj, proj, proj, dt_raw, dt_raw_t,
            conv_w, conv_w, conv_w, conv_b2, conv_b2, conv_b2,
            bias_p.reshape(1, hp), alog_p.reshape(1, hp), bias_p.reshape(hp, 1), alog_p.reshape(hp, 1),
            dskip_x, norm_g.astype(jnp.float32).reshape(1, d_inner))


def _idx_small_kernel(a_ref, w_ref, g_ref, b_ref, cos_ref, sa_ref, sb_ref, ki_ref, wi_ref, *, w_scale):
    acc = jnp.dot(a_ref[...], w_ref[...], preferred_element_type=jnp.float32)
    ki = acc[:, :IDX_HEAD_DIM]
    mu = jnp.mean(ki, axis=-1, keepdims=True)
    kc = ki - mu
    var = jnp.mean(kc * kc, axis=-1, keepdims=True)
    kn = kc * lax.rsqrt(var + LN_EPS) * g_ref[...] + b_ref[...]
    _rope_heads(kn, cos_ref[...], sa_ref[...], sb_ref[...], ki_ref)
    wi_ref[...] = acc[:, IDX_HEAD_DIM:] * w_scale


def _idx_small(xb, w_small, kn_g, kn_b, tables, tm=512):
    m, k = xb.shape
    tm = min(tm, m)
    w_scale = IDX_N_HEADS ** -0.5 * IDX_HEAD_DIM ** -0.5
    return pl.pallas_call(
        functools.partial(_idx_small_kernel, w_scale=w_scale),
        grid=(m // tm,),
        in_specs=[pl.BlockSpec((tm, k), lambda i: (i, 0)),
                  pl.BlockSpec((k, 2 * LANES), lambda i: (0, 0)),
                  pl.BlockSpec((1, LANES), lambda i: (0, 0)),
                  pl.BlockSpec((1, LANES), lambda i: (0, 0))]
                 + [pl.BlockSpec((tm, LANES), lambda i: (i, 0))] * 3,
        out_specs=[pl.BlockSpec((tm, LANES), lambda i: (i, 0)),
                   pl.BlockSpec((tm, LANES), lambda i: (i, 0))],
        out_shape=[jax.ShapeDtypeStruct((m, LANES), jnp.bfloat16),
                   jax.ShapeDtypeStruct((m, LANES), jnp.float32)],
        compiler_params=_params(("parallel",)),
        name="idx_small")(xb, w_small, kn_g.reshape(1, -1).astype(jnp.float32),
                          kn_b.reshape(1, -1).astype(jnp.float32), *tables)


def _index_select_kernel(qi_ref, wi_ref, kit_ref, bias_ref, keys_ref, *, tq, tk, ck, topk):
    i = pl.program_id(0)
    l = kit_ref.shape[1]
    n_kt = ((i + 1) * tq + tk - 1) // tk
    q_chunk = (i * tq + lax.broadcasted_iota(jnp.int32, (tq, tk), 0)) >> CHUNK_SHIFT
    wi = wi_ref[...]

    def score_tile(jt, carry):
        k0 = pl.multiple_of(jt * tk, tk)
        kt = kit_ref[:, pl.ds(k0, tk)]
        s = jnp.zeros((tq, tk), jnp.float32)
        for h in range(IDX_N_HEADS):
            d = jnp.dot(qi_ref[:, h * LANES:(h + 1) * LANES], kt, preferred_element_type=jnp.float32)
            s = s + wi[:, h:h + 1] * jnp.maximum(d, 0.0)
        k_chunk = (k0 + lax.broadcasted_iota(jnp.int32, (tq, tk), 1)) >> CHUNK_SHIFT
        bits = lax.bitcast_convert_type(s + 0.0, jnp.int32)
        key = jnp.where(bits < 0, bits ^ jnp.int32(0x7FFFFFFF), bits)
        keys_ref[:, pl.ds(k0, tk)] = jnp.where(k_chunk <= q_chunk, key, jnp.int32(INT_MIN))
        return carry

    lax.fori_loop(0, n_kt, score_tile, 0)

    n_ct = (n_kt * tk + ck - 1) // ck

    def pad_tile(jt, carry):
        keys_ref[:, pl.ds(pl.multiple_of(jt * tk, tk), tk)] = jnp.full((tq, tk), INT_MIN, jnp.int32)
        return carry

    lax.fori_loop(n_kt, n_ct * (ck // tk), pad_tile, 0)

    def bit_step(it, t):
        cand = t ^ lax.shift_left(jnp.int32(1), jnp.int32(31) - it)

        def count_tile(jt, cnt):
            blk = keys_ref[:, pl.ds(pl.multiple_of(jt * ck, ck), ck)]
            ge = jnp.where(blk >= cand, jnp.int32(1), jnp.int32(0))
            for cblk in range(ck // LANES):
                cnt = cnt + ge[:, cblk * LANES:(cblk + 1) * LANES]
            return cnt

        cnt = lax.fori_loop(0, n_ct, count_tile, jnp.zeros((tq, LANES), jnp.int32))
        total = jnp.sum(cnt.astype(jnp.float32), axis=1, keepdims=True)
        return jnp.where(total >= float(topk), cand, t)

    t = lax.fori_loop(0, 32, bit_step, jnp.full((tq, 1), INT_MIN, jnp.int32))
    t = jnp.maximum(t, jnp.int32(INT_MIN + 1))

    def write_tile(jt, carry):
        k0 = pl.multiple_of(jt * tk, tk)
        sel = keys_ref[:, pl.ds(k0, tk)] >= t
        bias_ref[0, pl.ds(k0, tk), :] = jnp.where(sel, 0.0, NEG_BIG).T.astype(bias_ref.dtype)
        return carry

    lax.fori_loop(0, n_kt, write_tile, 0)

    def fill_tile(jt, carry):
        k0 = pl.multiple_of(jt * tk, tk)
        bias_ref[0, pl.ds(k0, tk), :] = jnp.full((tk, tq), NEG_BIG, bias_ref.dtype)
        return carry

    lax.fori_loop(n_kt, l // tk, fill_tile, 0)


def _index_select(proj, wi, ki_t, topk, tq_out, tq=128, tk=256, ck=1024):
    l = proj.shape[0]
    tq, tk, ck = min(tq, l), min(tk, l), min(ck, l)
    qw = IDX_N_HEADS * IDX_HEAD_DIM
    r = tq_out // tq
    assert tq_out % tq == 0 and ck % tk == 0 and l % ck == 0
    return pl.pallas_call(
        functools.partial(_index_select_kernel, tq=tq, tk=tk, ck=ck, topk=topk),
        grid=(l // tq,),
        in_specs=[pl.BlockSpec((tq, qw), lambda i: (i, 0)),
                  pl.BlockSpec((tq, LANES), lambda i: (i, 0)),
                  pl.BlockSpec((IDX_HEAD_DIM, l), lambda i: (0, 0))],
        out_specs=pl.BlockSpec((1, l, tq), lambda i: (i // r, 0, i % r)),
        out_shape=jax.ShapeDtypeStruct((l // tq_out, l, tq_out), jnp.bfloat16),
        scratch_shapes=[pltpu.VMEM((tq, l), jnp.int32)],
        compiler_params=_params(("parallel",)),
        name="index_select")(proj, wi, ki_t)


ONES_ROWS = 16


def _attn_kernel(qt_ref, z_ref, k_ref, vt_ref, bias_ref, o_ref, m_ref, l_ref, acc_ref, *, tq, tk, group):
    i = pl.program_id(0)
    j = pl.program_id(1)
    n_kt = ((i + 1) * tq + tk - 1) // tk

    @pl.when(j == 0)
    def _():
        m_ref[...] = jnp.full_like(m_ref, NEG_BIG)
        l_ref[...] = jnp.zeros_like(l_ref)
        acc_ref[...] = jnp.zeros_like(acc_ref)

    @pl.when(j < n_kt)
    def _():
        bias = bias_ref[0].astype(jnp.float32)
        bias_g = jnp.concatenate([bias] * group, axis=1)
        ones = jnp.ones((ONES_ROWS, tk), jnp.bfloat16)
        for kv in range(ATT_N_KV):
            k_g = k_ref[:, kv * LANES:(kv + 1) * LANES]
            qt_g = jnp.concatenate(
                [qt_ref[(kv * group + hh) * LANES:(kv * group + hh + 1) * LANES, :] for hh in range(group)],
                axis=1)
            s = jnp.dot(k_g, qt_g, preferred_element_type=jnp.float32) + bias_g
            m_old = m_ref[kv]
            m_new = jnp.maximum(m_old, jnp.max(s, axis=0, keepdims=True))
            p = jnp.exp2(s - m_new[0:1, :]).astype(jnp.bfloat16)
            alpha = jnp.exp2(m_old - m_new)
            v_ext = jnp.concatenate([vt_ref[kv * LANES:(kv + 1) * LANES, :], ones], axis=0)
            pv = jnp.dot(v_ext, p, preferred_element_type=jnp.float32)
            m_ref[kv] = m_new
            l_ref[kv] = alpha * l_ref[kv] + pv[LANES:LANES + SUBLANES, :]
            for hh in range(group):
                h = kv * group + hh
                acc_ref[h * LANES:(h + 1) * LANES, :] = (
                    alpha[0:1, hh * tq:(hh + 1) * tq] * acc_ref[h * LANES:(h + 1) * LANES, :]
                    + pv[0:LANES, hh * tq:(hh + 1) * tq])

    @pl.when(j == n_kt - 1)
    def _():
        for h in range(ATT_N_KV * group):
            kv, hh = divmod(h, group)
            l_h = l_ref[kv][0:1, hh * tq:(hh + 1) * tq]
            o_t = acc_ref[h * LANES:(h + 1) * LANES, :] * (1.0 / l_h)
            zg = _silu(z_ref[:, h * LANES:(h + 1) * LANES].astype(jnp.float32))
            o_ref[:, h * LANES:(h + 1) * LANES] = (o_t.T * zg).astype(o_ref.dtype)


def _attention(proj, q_t, v_t, bias_t, z_blk, k_blk, d_att, tq, tk=512):
    l = proj.shape[0]
    tk = min(tk, l)
    n_heads = d_att // ATT_HEAD_DIM
    group = n_heads // ATT_N_KV
    kvw = ATT_N_KV * ATT_HEAD_DIM

    def tile(i, j):
        return jnp.minimum(j, ((i + 1) * tq + tk - 1) // tk - 1)

    return pl.pallas_call(
        functools.partial(_attn_kernel, tq=tq, tk=tk, group=group),
        grid=(l // tq, l // tk),
        in_specs=[pl.BlockSpec((d_att, tq), lambda i, j: (0, i)),
                  pl.BlockSpec((tq, d_att), lambda i, j: (i, z_blk)),
                  pl.BlockSpec((tk, kvw), lambda i, j: (tile(i, j), k_blk)),
                  pl.BlockSpec((kvw, tk), lambda i, j: (0, tile(i, j))),
                  pl.BlockSpec((1, tk, tq), lambda i, j: (i, tile(i, j), 0))],
        out_specs=pl.BlockSpec((tq, d_att), lambda i, j: (i, 0)),
        out_shape=jax.ShapeDtypeStruct((l, d_att), jnp.bfloat16),
        scratch_shapes=[pltpu.VMEM((ATT_N_KV, SUBLANES, group * tq), jnp.float32),
                        pltpu.VMEM((ATT_N_KV, SUBLANES, group * tq), jnp.float32),
                        pltpu.VMEM((d_att, tq), jnp.float32)],
        compiler_params=_params(("parallel", "arbitrary")),
        name="sparse_attention")(q_t, proj, proj, v_t, bias_t)


def _rope_tables(l):
    half = ROPE_DIM // 2
    inv = jnp.power(ROPE_THETA, -2.0 * jnp.arange(half, dtype=jnp.float32) / ROPE_DIM)
    ang = jnp.arange(l, dtype=jnp.float32)[:, None] * inv[None, :]
    cos, sin = jnp.cos(ang), jnp.sin(ang)
    zeros = jnp.zeros((l, LANES - ROPE_DIM), jnp.float32)
    zh = jnp.zeros((l, half), jnp.float32)
    cos_t = jnp.concatenate([cos, cos, jnp.ones_like(zeros)], axis=1)
    sa_t = jnp.concatenate([-sin, zh, zeros], axis=1)
    sb_t = jnp.concatenate([zh, sin, zeros], axis=1)
    return cos_t, sa_t, sb_t


def _ssd_layer(x_f32, x_bf, in_w, conv_w, conv_b, dt_bias, a_log, d_skip, norm_g, out_w, ln_g, ln_b, alpha):
    d = x_bf.shape[1]
    d_inner = out_w.shape[0]
    n_heads = d_inner // SSD_HEAD_DIM
    main_w = 2 * d_inner + 2 * SSD_N_GROUPS * SSD_D_STATE
    hp = -(-n_heads // LANES) * LANES
    w_main = in_w[:, :main_w].astype(jnp.bfloat16)
    w_dt = jnp.pad(in_w[:, main_w:], ((0, 0), (0, hp - n_heads))).astype(jnp.bfloat16)
    proj = _matmul(x_bf, w_main, jnp.bfloat16, 1024, 1024)
    dt_raw = _matmul(x_bf, w_dt, jnp.float32, 1024, LANES)
    y = _ssd_core(proj, dt_raw, dt_raw.T, conv_w, conv_b, dt_bias, a_log, d_skip, norm_g, d_inner)
    return _out_proj_ln(y, out_w.astype(jnp.bfloat16), x_f32, ln_g, ln_b, alpha)


def _dsa_layer(x_f32, x_bf, in_w, kn_g, kn_b, out_w, ln_g, ln_b, alpha, tables):
    l, d = x_bf.shape
    d_att = out_w.shape[0]
    kvw = ATT_N_KV * ATT_HEAD_DIM
    qw = IDX_N_HEADS * IDX_HEAD_DIM
    o_q, o_k, o_v, o_z = 0, d_att, d_att + kvw, d_att + 2 * kvw
    o_qi = o_z + d_att
    o_ki = o_qi + qw
    o_wi = o_ki + IDX_HEAD_DIM
    tn = 1024
    assert d_att % tn == 0 and qw % d_att == 0
    w_main = jnp.concatenate([in_w[:, o_qi:o_ki], in_w[:, o_q:o_k], in_w[:, o_z:o_qi],
                              in_w[:, o_k:o_v], in_w[:, o_v:o_z]], axis=1).astype(jnp.bfloat16)
    w_small = jnp.pad(in_w[:, o_ki:], ((0, 0), (0, 2 * LANES - IDX_HEAD_DIM - IDX_N_HEADS))).astype(jnp.bfloat16)
    c_q, c_z, c_k, c_v = qw, qw + d_att, qw + 2 * d_att, qw + 2 * d_att + kvw
    rope_ranges = ((0, (c_q + d_att) // tn), (c_k // tn, c_v // tn))
    q_scale = ATT_HEAD_DIM ** -0.5 * math.log2(math.e)
    proj = _matmul(x_bf, w_main, jnp.bfloat16, 1024, tn,
                   rope=(tables, rope_ranges, (c_q // tn, (c_q + d_att) // tn), q_scale))
    ki, wi = _idx_small(x_bf, w_small, kn_g, kn_b, tables)
    topk = min(TOPK_MAX, l // 4)
    tq_att = min(256, l)
    bias_t = _index_select(proj, wi, ki.T, topk, tq_att)
    q_t = proj[:, c_q:c_q + d_att].T
    v_t = proj[:, c_v:c_v + kvw].T
    o = _attention(proj, q_t, v_t, bias_t, c_z // d_att, c_k // kvw, d_att, tq_att)
    return _out_proj_ln(o, out_w.astype(jnp.bfloat16), x_f32, ln_g, ln_b, alpha)


def kernel(x, ssd_in_w, ssd_conv_w, ssd_conv_b, ssd_dt_bias, ssd_a_log, ssd_d_skip, ssd_norm_g, ssd_out_w,
           dsa_in_w, dsa_kn_g, dsa_kn_b, dsa_out_w, ln_g, ln_b):
    bsz, l, d = x.shape
    depth = ln_g.shape[0]
    alpha = (2 * depth) ** 0.25
    tables = _rope_tables(l)
    outs = []
    for bi in range(bsz):
        xf = x[bi].astype(jnp.float32)
        xb = xf.astype(jnp.bfloat16)
        for i in range(depth):
            j = i // 2
            if i % 2 == 0:
                xf, xb = _ssd_layer(xf, xb, ssd_in_w[j], ssd_conv_w[j], ssd_conv_b[j], ssd_dt_bias[j],
                                    ssd_a_log[j], ssd_d_skip[j], ssd_norm_g[j], ssd_out_w[j],
                                    ln_g[i], ln_b[i], alpha)
            else:
                xf, xb = _dsa_layer(xf, xb, dsa_in_w[j], dsa_kn_g[j], dsa_kn_b[j], dsa_out_w[j],
                                    ln_g[i], ln_b[i], alpha, tables)
        outs.append(xf)
    return jnp.stack(outs).astype(x.dtype)
```

```python
import functools
import math

import jax
import jax.numpy as jnp
from jax import lax
from jax.experimental import pallas as pl
from jax.experimental.pallas import tpu as pltpu

CHUNK = 64
CHUNK_SHIFT = 6
SSD_HEAD_SHIFT = 6
SSD_HEAD_DIM = 64
SSD_D_STATE = 128
SSD_N_GROUPS = 8
SSD_CONV = 4
SSD_CHUNK = 128
ATT_HEAD_DIM = 128
ATT_N_KV = 8
IDX_N_HEADS = 64
IDX_HEAD_DIM = 128
TOPK_MAX = 256
ROPE_THETA = 500000.0
ROPE_DIM = 32
LN_EPS = 1e-5
RMS_EPS = 1e-5

LANES = 128
SUBLANES = 8
V7X_VMEM_BYTES = 64 * 1024 * 1024
VMEM_LIMIT = V7X_VMEM_BYTES - 8 * 1024 * 1024

NEG_BIG = -1e30
INT_MIN = -2 ** 31

_HI = lax.Precision.HIGHEST


def _params(sem):
    return pltpu.CompilerParams(dimension_semantics=sem, vmem_limit_bytes=VMEM_LIMIT)


def _silu(v):
    return v * (1.0 / (1.0 + jnp.exp(-v)))


def _softplus(v):
    return jnp.maximum(v, 0.0) + jnp.log1p(jnp.exp(-jnp.abs(v)))


def _mm_kernel(a_ref, b_ref, o_ref):
    o_ref[...] = jnp.dot(a_ref[...], b_ref[...],
                         preferred_element_type=jnp.float32).astype(o_ref.dtype)


def _rope_heads(acc, cos, sa, sb, o_ref):
    for h in range(acc.shape[1] // LANES):
        a = acc[:, h * LANES:(h + 1) * LANES]
        r = (a * cos + pltpu.roll(a, LANES - ROPE_DIM // 2, 1) * sa
             + pltpu.roll(a, ROPE_DIM // 2, 1) * sb)
        o_ref[:, h * LANES:(h + 1) * LANES] = r.astype(o_ref.dtype)


def _mm_rope_kernel(a_ref, b_ref, cos_ref, sa_ref, sb_ref, o_ref):
    acc = jnp.dot(a_ref[...], b_ref[...], preferred_element_type=jnp.float32)
    _rope_heads(acc, cos_ref[0], sa_ref[0], sb_ref[0], o_ref)


def _matmul(a, b, out_dtype, tm, tn, rope=None):
    m, k = a.shape
    n = b.shape[1]
    tm, tn = min(tm, m), min(tn, n)
    assert m % tm == 0 and n % tn == 0
    in_specs = [pl.BlockSpec((tm, k), lambda i, j: (i, 0)),
                pl.BlockSpec((k, tn), lambda i, j: (0, j))]
    args = [a, b]
    if rope is None:
        body = _mm_kernel
    else:
        tables, kind_of_tile = rope
        body = _mm_rope_kernel
        in_specs += [pl.BlockSpec((1, tm, LANES), lambda i, j: (kind_of_tile(j), i, 0))] * 3
        args += list(tables)
    return pl.pallas_call(
        body, grid=(m // tm, n // tn), in_specs=in_specs,
        out_specs=pl.BlockSpec((tm, tn), lambda i, j: (i, j)),
        out_shape=jax.ShapeDtypeStruct((m, n), out_dtype),
        compiler_params=_params(("parallel", "parallel")),
        name="matmul_rope" if rope is not None else "matmul")(*args)


LN_ROWS = 64


def _mm_ln_kernel(y_ref, w_ref, x_ref, g_ref, b_ref, o_ref, obf_ref, *, alpha, nk, n_split):
    k = pl.program_id(1)
    tm, d = o_ref.shape
    dn = d // n_split

    @pl.when(k == 0)
    def _():
        o_ref[...] = alpha * x_ref[...]

    for c in range(n_split):
        cols = slice(c * dn, (c + 1) * dn)
        o_ref[:, cols] += jnp.dot(y_ref[...], w_ref[:, cols], preferred_element_type=jnp.float32)

    @pl.when(k == nk - 1)
    def _():
        def ln_rows(r, carry):
            rows = pl.ds(pl.multiple_of(r * LN_ROWS, LN_ROWS), LN_ROWS)
            h = o_ref[rows, :]
            mu = jnp.mean(h, axis=-1, keepdims=True)
            hc = h - mu
            var = jnp.mean(hc * hc, axis=-1, keepdims=True)
            out = hc * lax.rsqrt(var + LN_EPS) * g_ref[...] + b_ref[...]
            o_ref[rows, :] = out
            obf_ref[rows, :] = out.astype(obf_ref.dtype)
            return carry

        lax.fori_loop(0, tm // LN_ROWS, ln_rows, 0)


def _out_proj_ln(y, w, x, g, b, alpha, tm=512, tk=512):
    m, kdim = y.shape
    d = w.shape[1]
    tm, tk = min(tm, m), min(tk, kdim)
    nk = kdim // tk
    n_split = 2 if d % (2 * LANES) == 0 else 1
    assert m % tm == 0 and kdim % tk == 0 and tm % LN_ROWS == 0
    return pl.pallas_call(
        functools.partial(_mm_ln_kernel, alpha=alpha, nk=nk, n_split=n_split),
        grid=(m // tm, nk),
        in_specs=[pl.BlockSpec((tm, tk), lambda i, k: (i, k)),
                  pl.BlockSpec((tk, d), lambda i, k: (k, 0)),
                  pl.BlockSpec((tm, d), lambda i, k: (i, 0)),
                  pl.BlockSpec((1, d), lambda i, k: (0, 0)),
                  pl.BlockSpec((1, d), lambda i, k: (0, 0))],
        out_specs=[pl.BlockSpec((tm, d), lambda i, k: (i, 0)),
                   pl.BlockSpec((tm, d), lambda i, k: (i, 0))],
        out_shape=[jax.ShapeDtypeStruct((m, d), jnp.float32),
                   jax.ShapeDtypeStruct((m, d), jnp.bfloat16)],
        compiler_params=_params(("parallel", "arbitrary")),
        name="out_proj_ln")(y, w, x, g.reshape(1, d), b.reshape(1, d))


def _ssd_kernel(xr_ref, br_ref, cr_ref, z_ref, dt_ref, dtt_ref,
                cwx_ref, cwb_ref, cwc_ref, cbx_ref, cbb_ref, cbc_ref,
                bias_r_ref, alog_r_ref, bias_c_ref, alog_c_ref, dskip_ref, ng_ref,
                y_ref, xbuf, bbuf, cbuf, acsr_ref, h_ref, *, e_heads):
    g = pl.program_id(0)
    c = pl.program_id(1)
    q = SSD_CHUNK
    gw = xbuf.shape[1]
    hp = dt_ref.shape[1]
    tail = SUBLANES

    @pl.when(c == 0)
    def _():
        xbuf[0:tail, :] = jnp.zeros((tail, gw), jnp.float32)
        bbuf[0:tail, :] = jnp.zeros((tail, SSD_D_STATE), jnp.float32)
        cbuf[0:tail, :] = jnp.zeros((tail, SSD_D_STATE), jnp.float32)
        h_ref[...] = jnp.zeros_like(h_ref)

    def conv_silu(buf, raw_ref, w_ref, b_ref):
        buf[tail:tail + q, :] = raw_ref[...].astype(jnp.float32)
        w = w_ref[...]
        acc = b_ref[...] + w[0:1, :] * buf[tail - 3:tail - 3 + q, :]
        for j in range(1, SSD_CONV):
            acc = acc + w[j:j + 1, :] * buf[tail - 3 + j:tail - 3 + j + q, :]
        buf[0:tail, :] = buf[q:q + tail, :]
        return _silu(acc)

    xs = conv_silu(xbuf, xr_ref, cwx_ref, cbx_ref)
    bm = conv_silu(bbuf, br_ref, cwb_ref, cbb_ref)
    cm = conv_silu(cbuf, cr_ref, cwc_ref, cbc_ref)

    dt_c = _softplus(dt_ref[...] + bias_r_ref[...])
    da_c = dt_c * (-jnp.exp(alog_r_ref[...]))
    dt_r = _softplus(dtt_ref[...] + bias_c_ref[...])
    da_r = dt_r * (-jnp.exp(alog_c_ref[...]))
    ri = lax.broadcasted_iota(jnp.int32, (q, q), 0)
    ci = lax.broadcasted_iota(jnp.int32, (q, q), 1)
    causal = ri >= ci
    tril = causal.astype(jnp.float32)
    triu = (ri <= ci).astype(jnp.float32)
    acs_c = jnp.dot(tril, da_c, precision=_HI, preferred_element_type=jnp.float32)
    acsr_ref[...] = jnp.dot(da_r, triu, precision=_HI, preferred_element_type=jnp.float32)

    eh = lax.broadcasted_iota(jnp.int32, (hp, gw), 0)
    ec = lax.broadcasted_iota(jnp.int32, (hp, gw), 1)
    expand = (eh == g * e_heads + (ec >> SSD_HEAD_SHIFT)).astype(jnp.bfloat16)
    both = jnp.concatenate([dt_c, acs_c], axis=0)
    hi = both.astype(jnp.bfloat16)
    rest = both - hi.astype(jnp.float32)
    mid = rest.astype(jnp.bfloat16)
    low = (rest - mid.astype(jnp.float32)).astype(jnp.bfloat16)
    both_x = jnp.dot(jnp.concatenate([hi, mid, low], axis=1),
                     jnp.concatenate([expand, expand, expand], axis=0),
                     preferred_element_type=jnp.float32)
    dt_x = both_x[:q, :]
    acs_x = both_x[q:, :]
    last_x = acs_x[q - 1:q, :]
    x_dt = xs * dt_x
    x_dt_bf = x_dt.astype(jnp.bfloat16)
    x_dec_bf = (x_dt * jnp.exp(last_x - acs_x)).astype(jnp.bfloat16)

    bt = bm.T
    bt_bf = bt.astype(jnp.bfloat16)
    cm_bf = cm.astype(jnp.bfloat16)
    cb = jnp.dot(cm_bf, bt_bf, preferred_element_type=jnp.float32)

    h_prev = h_ref[...]
    y_off = jnp.dot(cm_bf, h_prev.astype(jnp.bfloat16), preferred_element_type=jnp.float32)
    y = y_off * jnp.exp(acs_x)
    s_new = jnp.dot(bt_bf, x_dec_bf, preferred_element_type=jnp.float32)
    h_ref[...] = jnp.exp(last_x) * h_prev + s_new

    lane = lax.broadcasted_iota(jnp.int32, (q, LANES), 1)
    lo = lane < SSD_HEAD_DIM
    y_parts = []
    for p in range(gw // LANES):
        col = acs_x[:, p * LANES:(p + 1) * LANES]
        col_sw = pltpu.roll(col, SSD_HEAD_DIM, 1)
        xp = x_dt_bf[:, p * LANES:(p + 1) * LANES]
        ws = []
        for half in range(2):
            colb = jnp.where(lo, col, col_sw) if half == 0 else jnp.where(lo, col_sw, col)
            hg = g * e_heads + 2 * p + half
            rowb = acsr_ref[pl.ds(hg, 1), :]
            seg = jnp.where(causal, colb - rowb, NEG_BIG)
            ws.append((cb * jnp.exp(seg)).astype(jnp.bfloat16))
        w2 = jnp.concatenate(ws, axis=1)
        zero = jnp.zeros_like(xp)
        x2 = jnp.concatenate([jnp.where(lo, xp, zero), jnp.where(lo, zero, xp)], axis=0)
        y_parts.append(jnp.dot(w2, x2, preferred_element_type=jnp.float32))
    y = y + jnp.concatenate(y_parts, axis=1)

    y = y + dskip_ref[...] * xs
    y = y * _silu(z_ref[...].astype(jnp.float32))
    ms = jnp.mean(y * y, axis=-1, keepdims=True)
    y_ref[...] = (y * lax.rsqrt(ms + RMS_EPS) * ng_ref[...]).astype(y_ref.dtype)


def _ssd_core(proj, dt_raw, dt_raw_t, conv_w, conv_b, dt_bias, a_log, d_skip, norm_g, d_inner):
    l = proj.shape[0]
    hp = dt_raw.shape[1]
    g_n = SSD_N_GROUPS
    gw = d_inner // g_n
    n = SSD_D_STATE
    e_heads = gw // SSD_HEAD_DIM
    n_heads = d_inner // SSD_HEAD_DIM
    q = SSD_CHUNK
    nc = l // q
    assert l % q == 0 and gw % LANES == 0

    def pad_h(v):
        return jnp.pad(v.astype(jnp.float32), (0, hp - n_heads))

    bias_p, alog_p = pad_h(dt_bias), pad_h(a_log)
    dskip_x = jnp.repeat(d_skip.astype(jnp.float32), SSD_HEAD_DIM).reshape(1, d_inner)
    xb = d_inner // gw
    bb = 2 * d_inner // n
    cbk = (2 * d_inner + g_n * n) // n
    cwb = d_inner // n
    cwc = (d_inner + g_n * n) // n
    conv_b2 = conv_b.reshape(1, -1)

    return pl.pallas_call(
        functools.partial(_ssd_kernel, e_heads=e_heads),
        grid=(g_n, nc),
        in_specs=[
            pl.BlockSpec((q, gw), lambda g, c: (c, xb + g)),
            pl.BlockSpec((q, n), lambda g, c: (c, bb + g)),
            pl.BlockSpec((q, n), lambda g, c: (c, cbk + g)),
            pl.BlockSpec((q, gw), lambda g, c: (c, g)),
            pl.BlockSpec((q, hp), lambda g, c: (c, 0)),
            pl.BlockSpec((hp, q), lambda g, c: (0, c)),
            pl.BlockSpec((SSD_CONV, gw), lambda g, c: (0, g)),
            pl.BlockSpec((SSD_CONV, n), lambda g, c: (0, cwb + g)),
            pl.BlockSpec((SSD_CONV, n), lambda g, c: (0, cwc + g)),
            pl.BlockSpec((1, gw), lambda g, c: (0, g)),
            pl.BlockSpec((1, n), lambda g, c: (0, cwb + g)),
            pl.BlockSpec((1, n), lambda g, c: (0, cwc + g)),
            pl.BlockSpec((1, hp), lambda g, c: (0, 0)),
            pl.BlockSpec((1, hp), lambda g, c: (0, 0)),
            pl.BlockSpec((hp, 1), lambda g, c: (0, 0)),
            pl.BlockSpec((hp, 1), lambda g, c: (0, 0)),
            pl.BlockSpec((1, gw), lambda g, c: (0, g)),
            pl.BlockSpec((1, gw), lambda g, c: (0, g)),
        ],
        out_specs=pl.BlockSpec((q, gw), lambda g, c: (c, g)),
        out_shape=jax.ShapeDtypeStruct((l, d_inner), jnp.bfloat16),
        scratch_shapes=[pltpu.VMEM((q + SUBLANES, gw), jnp.float32),
                        pltpu.VMEM((q + SUBLANES, n), jnp.float32),
                        pltpu.VMEM((q + SUBLANES, n), jnp.float32),
                        pltpu.VMEM((hp, q), jnp.float32),
                        pltpu.VMEM((n, gw), jnp.float32)],
        compiler_params=_params(("parallel", "arbitrary")),
        name="ssd_core")(
            proj, proj, proj, proj, dt_raw, dt_raw_t,
            conv_w, conv_w, conv_w, conv_b2, conv_b2, conv_b2,
            bias_p.reshape(1, hp), alog_p.reshape(1, hp), bias_p.reshape(hp, 1), alog_p.reshape(hp, 1),
            dskip_x, norm_g.astype(jnp.float32).reshape(1, d_inner))


def _idx_small_kernel(a_ref, w_ref, g_ref, b_ref, cos_ref, sa_ref, sb_ref, ki_ref, wi_ref, *, w_scale):
    acc = jnp.dot(a_ref[...], w_ref[...], preferred_element_type=jnp.float32)
    ki = acc[:, :IDX_HEAD_DIM]
    mu = jnp.mean(ki, axis=-1, keepdims=True)
    kc = ki - mu
    var = jnp.mean(kc * kc, axis=-1, keepdims=True)
    kn = kc * lax.rsqrt(var + LN_EPS) * g_ref[...] + b_ref[...]
    _rope_heads(kn, cos_ref[...], sa_ref[...], sb_ref[...], ki_ref)
    wi_ref[...] = acc[:, IDX_HEAD_DIM:] * w_scale


def _idx_small(xb, w_small, kn_g, kn_b, tables, tm=512):
    m, k = xb.shape
    tm = min(tm, m)
    w_scale = IDX_N_HEADS ** -0.5 * IDX_HEAD_DIM ** -0.5
    return pl.pallas_call(
        functools.partial(_idx_small_kernel, w_scale=w_scale),
        grid=(m // tm,),
        in_specs=[pl.BlockSpec((tm, k), lambda i: (i, 0)),
                  pl.BlockSpec((k, 2 * LANES), lambda i: (0, 0)),
                  pl.BlockSpec((1, LANES), lambda i: (0, 0)),
                  pl.BlockSpec((1, LANES), lambda i: (0, 0))]
                 + [pl.BlockSpec((tm, LANES), lambda i: (i, 0))] * 3,
        out_specs=[pl.BlockSpec((tm, LANES), lambda i: (i, 0)),
                   pl.BlockSpec((tm, LANES), lambda i: (i, 0))],
        out_shape=[jax.ShapeDtypeStruct((m, LANES), jnp.bfloat16),
                   jax.ShapeDtypeStruct((m, LANES), jnp.float32)],
        compiler_params=_params(("parallel",)),
        name="idx_small")(xb, w_small, kn_g.reshape(1, -1).astype(jnp.float32),
                          kn_b.reshape(1, -1).astype(jnp.float32), *tables)


I16_MIN = -2 ** 15


def _index_select_kernel(qi_ref, wi_ref, kit_ref, bias_ref, keys_ref, half_ref, *, tq, tk, ck, topk):
    i = pl.program_id(0)
    l = kit_ref.shape[1]
    n_kt = ((i + 1) * tq + tk - 1) // tk
    q_chunk = (i * tq + lax.broadcasted_iota(jnp.int32, (tq, tk), 0)) >> CHUNK_SHIFT
    wi = wi_ref[...]

    def score_tile(jt, carry):
        k0 = pl.multiple_of(jt * tk, tk)
        kt = kit_ref[:, pl.ds(k0, tk)]
        s = jnp.zeros((tq, tk), jnp.float32)
        for h in range(IDX_N_HEADS):
            d = jnp.dot(qi_ref[:, h * LANES:(h + 1) * LANES], kt, preferred_element_type=jnp.float32)
            s = s + wi[:, h:h + 1] * jnp.maximum(d, 0.0)
        k_chunk = (k0 + lax.broadcasted_iota(jnp.int32, (tq, tk), 1)) >> CHUNK_SHIFT
        bits = lax.bitcast_convert_type(s + 0.0, jnp.int32)
        key = jnp.where(bits < 0, bits ^ jnp.int32(0x7FFFFFFF), bits)
        keys_ref[:, pl.ds(k0, tk)] = jnp.where(k_chunk <= q_chunk, key, jnp.int32(INT_MIN))
        return carry

    lax.fori_loop(0, n_kt, score_tile, 0)

    n_ct = (n_kt * tk + ck - 1) // ck

    def pad_tile(jt, carry):
        keys_ref[:, pl.ds(pl.multiple_of(jt * tk, tk), tk)] = jnp.full((tq, tk), INT_MIN, jnp.int32)
        return carry

    lax.fori_loop(n_kt, n_ct * (ck // tk), pad_tile, 0)

    def count16(cmp, thr_b):
        def count_tile(jt, cnt):
            blk = half_ref[:, pl.ds(pl.multiple_of(jt * ck, ck), ck)]
            for cblk in range(ck // LANES):
                hit = cmp(blk[:, cblk * LANES:(cblk + 1) * LANES], thr_b)
                cnt = cnt + jnp.where(hit, jnp.int16(1), jnp.int16(0))
            return cnt

        cnt = lax.fori_loop(0, n_ct, count_tile, jnp.zeros((tq, LANES), jnp.int16))
        return jnp.sum(cnt.astype(jnp.float32), axis=1, keepdims=True)

    def lanes16(v):
        return jnp.broadcast_to(v, (tq, LANES)).astype(jnp.int16)

    def search16(need):
        def bit_step(it, t):
            bit = jnp.where(it == 0, jnp.int32(I16_MIN), lax.shift_left(jnp.int32(1), jnp.int32(15) - it))
            cand = t ^ bit
            total = count16(lambda a, b: a >= b, lanes16(cand))
            return jnp.where(total >= need, cand, t)

        return lax.fori_loop(0, 16, bit_step, jnp.full((tq, 1), I16_MIN, jnp.int32))

    def to_high_half(jt, carry):
        cols = pl.ds(pl.multiple_of(jt * ck, ck), ck)
        half_ref[:, cols] = (keys_ref[:, cols] >> 16).astype(jnp.int16)
        return carry

    lax.fori_loop(0, n_ct, to_high_half, 0)
    t_hi = search16(jnp.float32(topk))
    above = count16(lambda a, b: a > b, lanes16(t_hi))

    def to_low_half(jt, carry):
        cols = pl.ds(pl.multiple_of(jt * ck, ck), ck)
        key = keys_ref[:, cols]
        low = (key & jnp.int32(0xFFFF)) + jnp.int32(I16_MIN)
        half_ref[:, cols] = jnp.where((key >> 16) == t_hi, low, jnp.int32(I16_MIN)).astype(jnp.int16)
        return carry

    lax.fori_loop(0, n_ct, to_low_half, 0)
    t_lo = search16(jnp.float32(topk) - above)
    t = lax.shift_left(t_hi, jnp.int32(16)) + (t_lo - jnp.int32(I16_MIN))
    t = jnp.maximum(t, jnp.int32(INT_MIN + 1))

    def write_tile(jt, carry):
        k0 = pl.multiple_of(jt * tk, tk)
        sel = keys_ref[:, pl.ds(k0, tk)] >= t
        bias_ref[0, pl.ds(k0, tk), :] = jnp.where(sel, 0.0, NEG_BIG).T.astype(bias_ref.dtype)
        return carry

    lax.fori_loop(0, n_kt, write_tile, 0)

    def fill_tile(jt, carry):
        k0 = pl.multiple_of(jt * tk, tk)
        bias_ref[0, pl.ds(k0, tk), :] = jnp.full((tk, tq), NEG_BIG, bias_ref.dtype)
        return carry

    lax.fori_loop(n_kt, l // tk, fill_tile, 0)


def _index_select(proj, wi, ki_t, topk, tq_out, tq=128, tk=256, ck=1024):
    l = proj.shape[0]
    tq, tk, ck = min(tq, l), min(tk, l), min(ck, l)
    qw = IDX_N_HEADS * IDX_HEAD_DIM
    r = tq_out // tq
    assert tq_out % tq == 0 and ck % tk == 0 and l % ck == 0
    return pl.pallas_call(
        functools.partial(_index_select_kernel, tq=tq, tk=tk, ck=ck, topk=topk),
        grid=(l // tq,),
        in_specs=[pl.BlockSpec((tq, qw), lambda i: (i, 0)),
                  pl.BlockSpec((tq, LANES), lambda i: (i, 0)),
                  pl.BlockSpec((IDX_HEAD_DIM, l), lambda i: (0, 0))],
        out_specs=pl.BlockSpec((1, l, tq), lambda i: (i // r, 0, i % r)),
        out_shape=jax.ShapeDtypeStruct((l // tq_out, l, tq_out), jnp.bfloat16),
        scratch_shapes=[pltpu.VMEM((tq, l), jnp.int32), pltpu.VMEM((tq, l), jnp.int16)],
        compiler_params=_params(("parallel",)),
        name="index_select")(proj, wi, ki_t)


ONES_ROWS = 16


def _attn_kernel(qb_ref, kt_ref, qt_ref, z_ref, k_ref, vt_ref, bias_ref, o_ref, m_ref, l_ref, acc_ref,
                 *, tq, tk, group):
    step = pl.program_id(0)
    i = qb_ref[step]
    j = kt_ref[step]
    n_kt = ((i + 1) * tq + tk - 1) // tk

    @pl.when(j == 0)
    def _():
        m_ref[...] = jnp.full_like(m_ref, NEG_BIG)
        l_ref[...] = jnp.zeros_like(l_ref)
        acc_ref[...] = jnp.zeros_like(acc_ref)

    bias = bias_ref[0].astype(jnp.float32)
    bias_g = jnp.concatenate([bias] * group, axis=1)
    ones = jnp.ones((ONES_ROWS, tk), jnp.bfloat16)

    def logits(kv):
        k_g = k_ref[:, kv * LANES:(kv + 1) * LANES]
        qt_g = jnp.concatenate(
            [qt_ref[(kv * group + hh) * LANES:(kv * group + hh + 1) * LANES, :] for hh in range(group)],
            axis=1)
        return jnp.dot(k_g, qt_g, preferred_element_type=jnp.float32) + bias_g

    def softmax(kv, s):
        m_old = m_ref[kv]
        m_new = jnp.maximum(m_old, jnp.max(s, axis=0, keepdims=True))
        m_ref[kv] = m_new
        return jnp.exp2(s - m_new[0:1, :]).astype(jnp.bfloat16), jnp.exp2(m_old - m_new)

    def accumulate(kv, p, alpha):
        v_ext = jnp.concatenate([vt_ref[kv * LANES:(kv + 1) * LANES, :], ones], axis=0)
        pv = jnp.dot(v_ext, p, preferred_element_type=jnp.float32)
        l_ref[kv] = alpha * l_ref[kv] + pv[LANES:LANES + SUBLANES, :]
        for hh in range(group):
            h = kv * group + hh
            acc_ref[h * LANES:(h + 1) * LANES, :] = (
                alpha[0:1, hh * tq:(hh + 1) * tq] * acc_ref[h * LANES:(h + 1) * LANES, :]
                + pv[0:LANES, hh * tq:(hh + 1) * tq])

    s_next = logits(0)
    for kv in range(ATT_N_KV):
        s = s_next
        if kv + 1 < ATT_N_KV:
            s_next = logits(kv + 1)
        p, alpha = softmax(kv, s)
        accumulate(kv, p, alpha)

    @pl.when(j == n_kt - 1)
    def _():
        for h in range(ATT_N_KV * group):
            kv, hh = divmod(h, group)
            l_h = l_ref[kv][0:1, hh * tq:(hh + 1) * tq]
            o_t = acc_ref[h * LANES:(h + 1) * LANES, :] * (1.0 / l_h)
            zg = _silu(z_ref[:, h * LANES:(h + 1) * LANES].astype(jnp.float32))
            o_ref[:, h * LANES:(h + 1) * LANES] = (o_t.T * zg).astype(o_ref.dtype)


def _attention(proj, q_t, v_t, bias_t, z_blk, k_blk, d_att, tq, tk=512):
    l = proj.shape[0]
    tk = min(tk, l)
    n_heads = d_att // ATT_HEAD_DIM
    group = n_heads // ATT_N_KV
    kvw = ATT_N_KV * ATT_HEAD_DIM

    pairs = [(i, j) for i in range(l // tq) for j in range(((i + 1) * tq + tk - 1) // tk)]
    qb = jnp.asarray([p[0] for p in pairs], jnp.int32)
    kt = jnp.asarray([p[1] for p in pairs], jnp.int32)
    grid_spec = pltpu.PrefetchScalarGridSpec(
        num_scalar_prefetch=2,
        grid=(len(pairs),),
        in_specs=[pl.BlockSpec((d_att, tq), lambda s, qb, kt: (0, qb[s])),
                  pl.BlockSpec((tq, d_att), lambda s, qb, kt: (qb[s], z_blk)),
                  pl.BlockSpec((tk, kvw), lambda s, qb, kt: (kt[s], k_blk)),
                  pl.BlockSpec((kvw, tk), lambda s, qb, kt: (0, kt[s])),
                  pl.BlockSpec((1, tk, tq), lambda s, qb, kt: (qb[s], kt[s], 0))],
        out_specs=pl.BlockSpec((tq, d_att), lambda s, qb, kt: (qb[s], 0)),
        scratch_shapes=[pltpu.VMEM((ATT_N_KV, SUBLANES, group * tq), jnp.float32),
                        pltpu.VMEM((ATT_N_KV, SUBLANES, group * tq), jnp.float32),
                        pltpu.VMEM((d_att, tq), jnp.float32)])
    return pl.pallas_call(
        functools.partial(_attn_kernel, tq=tq, tk=tk, group=group),
        grid_spec=grid_spec,
        out_shape=jax.ShapeDtypeStruct((l, d_att), jnp.bfloat16),
        compiler_params=_params(("arbitrary",)),
        name="sparse_attention")(qb, kt, q_t, proj, proj, v_t, bias_t)


def _rope_tables(l):
    half = ROPE_DIM // 2
    inv = jnp.power(ROPE_THETA, -2.0 * jnp.arange(half, dtype=jnp.float32) / ROPE_DIM)
    ang = jnp.arange(l, dtype=jnp.float32)[:, None] * inv[None, :]
    cos, sin = jnp.cos(ang), jnp.sin(ang)
    zeros = jnp.zeros((l, LANES - ROPE_DIM), jnp.float32)
    zh = jnp.zeros((l, half), jnp.float32)
    cos_t = jnp.concatenate([cos, cos, jnp.ones_like(zeros)], axis=1)
    sa_t = jnp.concatenate([-sin, zh, zeros], axis=1)
    sb_t = jnp.concatenate([zh, sin, zeros], axis=1)
    q_scale = ATT_HEAD_DIM ** -0.5 * math.log2(math.e)
    return (jnp.stack([cos_t, cos_t * q_scale, jnp.ones_like(cos_t)]),
            jnp.stack([sa_t, sa_t * q_scale, jnp.zeros_like(sa_t)]),
            jnp.stack([sb_t, sb_t * q_scale, jnp.zeros_like(sb_t)]))


def _ssd_layer(x_f32, x_bf, in_w, conv_w, conv_b, dt_bias, a_log, d_skip, norm_g, out_w, ln_g, ln_b, alpha):
    d = x_bf.shape[1]
    d_inner = out_w.shape[0]
    n_heads = d_inner // SSD_HEAD_DIM
    main_w = 2 * d_inner + 2 * SSD_N_GROUPS * SSD_D_STATE
    hp = -(-n_heads // LANES) * LANES
    w_main = in_w[:, :main_w].astype(jnp.bfloat16)
    w_dt = jnp.pad(in_w[:, main_w:], ((0, 0), (0, hp - n_heads))).astype(jnp.bfloat16)
    proj = _matmul(x_bf, w_main, jnp.bfloat16, 1024, 1024)
    dt_raw = _matmul(x_bf, w_dt, jnp.float32, 1024, LANES)
    y = _ssd_core(proj, dt_raw, dt_raw.T, conv_w, conv_b, dt_bias, a_log, d_skip, norm_g, d_inner)
    return _out_proj_ln(y, out_w.astype(jnp.bfloat16), x_f32, ln_g, ln_b, alpha)


def _dsa_layer(x_f32, x_bf, in_w, kn_g, kn_b, out_w, ln_g, ln_b, alpha, tables):
    l, d = x_bf.shape
    d_att = out_w.shape[0]
    kvw = ATT_N_KV * ATT_HEAD_DIM
    qw = IDX_N_HEADS * IDX_HEAD_DIM
    o_q, o_k, o_v, o_z = 0, d_att, d_att + kvw, d_att + 2 * kvw
    o_qi = o_z + d_att
    o_ki = o_qi + qw
    o_wi = o_ki + IDX_HEAD_DIM
    tn = 1024
    assert d_att % tn == 0 and qw % d_att == 0
    w_main = jnp.concatenate([in_w[:, o_qi:o_ki], in_w[:, o_q:o_k], in_w[:, o_z:o_qi],
                              in_w[:, o_k:o_v], in_w[:, o_v:o_z]], axis=1).astype(jnp.bfloat16)
    w_small = jnp.pad(in_w[:, o_ki:], ((0, 0), (0, 2 * LANES - IDX_HEAD_DIM - IDX_N_HEADS))).astype(jnp.bfloat16)
    c_q, c_z, c_k, c_v = qw, qw + d_att, qw + 2 * d_att, qw + 2 * d_att + kvw
    t_q, t_z, t_k, t_v = c_q // tn, c_z // tn, c_k // tn, c_v // tn

    def kind_of_tile(j):
        return jnp.where(j < t_q, 0, jnp.where(j < t_z, 1, jnp.where(jnp.logical_and(j >= t_k, j < t_v), 0, 2)))

    proj = _matmul(x_bf, w_main, jnp.bfloat16, 1024, tn, rope=(tables, kind_of_tile))
    ki, wi = _idx_small(x_bf, w_small, kn_g, kn_b, tuple(t[0] for t in tables))
    topk = min(TOPK_MAX, l // 4)
    tq_att = min(256, l)
    bias_t = _index_select(proj, wi, ki.T, topk, tq_att)
    q_t = proj[:, c_q:c_q + d_att].T
    v_t = proj[:, c_v:c_v + kvw].T
    o = _attention(proj, q_t, v_t, bias_t, c_z // d_att, c_k // kvw, d_att, tq_att)
    return _out_proj_ln(o, out_w.astype(jnp.bfloat16), x_f32, ln_g, ln_b, alpha)


def kernel(x, ssd_in_w, ssd_conv_w, ssd_conv_b, ssd_dt_bias, ssd_a_log, ssd_d_skip, ssd_norm_g, ssd_out_w,
           dsa_in_w, dsa_kn_g, dsa_kn_b, dsa_out_w, ln_g, ln_b):
    bsz, l, d = x.shape
    depth = ln_g.shape[0]
    alpha = (2 * depth) ** 0.25
    tables = _rope_tables(l)
    outs = []
    for bi in range(bsz):
        xf = x[bi].astype(jnp.float32)
        xb = xf.astype(jnp.bfloat16)
        for i in range(depth):
            j = i // 2
            if i % 2 == 0:
                xf, xb = _ssd_layer(xf, xb, ssd_in_w[j], ssd_conv_w[j], ssd_conv_b[j], ssd_dt_bias[j],
                                    ssd_a_log[j], ssd_d_skip[j], ssd_norm_g[j], ssd_out_w[j],
                                    ln_g[i], ln_b[i], alpha)
            else:
                xf, xb = _dsa_layer(xf, xb, dsa_in_w[j], dsa_kn_g[j], dsa_kn_b[j], dsa_out_w[j],
                                    ln_g[i], ln_b[i], alpha, tables)
        outs.append(xf)
    return jnp.stack(outs).astype(x.dtype)
```

```python
import functools
import math

import jax
import jax.numpy as jnp
from jax import lax
from jax.experimental import pallas as pl
from jax.experimental.pallas import tpu as pltpu

CHUNK = 64
CHUNK_SHIFT = 6
SSD_HEAD_SHIFT = 6
SSD_HEAD_DIM = 64
SSD_D_STATE = 128
SSD_N_GROUPS = 8
SSD_CONV = 4
SSD_CHUNK = 128
ATT_HEAD_DIM = 128
ATT_N_KV = 8
IDX_N_HEADS = 64
IDX_HEAD_DIM = 128
TOPK_MAX = 256
ROPE_THETA = 500000.0
ROPE_DIM = 32
LN_EPS = 1e-5
RMS_EPS = 1e-5

LANES = 128
SUBLANES = 8
V7X_VMEM_BYTES = 64 * 1024 * 1024
VMEM_LIMIT = V7X_VMEM_BYTES - 8 * 1024 * 1024

NEG_BIG = -1e30
INT_MIN = -2 ** 31

_HI = lax.Precision.HIGHEST


def _params(sem):
    return pltpu.CompilerParams(dimension_semantics=sem, vmem_limit_bytes=VMEM_LIMIT)


def _silu(v):
    return v * (1.0 / (1.0 + jnp.exp(-v)))


def _softplus(v):
    return jnp.maximum(v, 0.0) + jnp.log1p(jnp.exp(-jnp.abs(v)))


def _mm_kernel(a_ref, b_ref, o_ref):
    o_ref[...] = jnp.dot(a_ref[...], b_ref[...],
                         preferred_element_type=jnp.float32).astype(o_ref.dtype)


def _rope_heads(acc, cos, sa, sb, o_ref):
    for h in range(acc.shape[1] // LANES):
        a = acc[:, h * LANES:(h + 1) * LANES]
        r = (a * cos + pltpu.roll(a, LANES - ROPE_DIM // 2, 1) * sa
             + pltpu.roll(a, ROPE_DIM // 2, 1) * sb)
        o_ref[:, h * LANES:(h + 1) * LANES] = r.astype(o_ref.dtype)


def _mm_rope_kernel(a_ref, b_ref, cos_ref, sa_ref, sb_ref, o_ref):
    acc = jnp.dot(a_ref[...], b_ref[...], preferred_element_type=jnp.float32)
    _rope_heads(acc, cos_ref[0], sa_ref[0], sb_ref[0], o_ref)


def _matmul(a, b, out_dtype, tm, tn, rope=None):
    m, k = a.shape
    n = b.shape[1]
    tm, tn = min(tm, m), min(tn, n)
    assert m % tm == 0 and n % tn == 0
    in_specs = [pl.BlockSpec((tm, k), lambda i, j: (i, 0)),
                pl.BlockSpec((k, tn), lambda i, j: (0, j))]
    args = [a, b]
    if rope is None:
        body = _mm_kernel
    else:
        tables, kind_of_tile = rope
        body = _mm_rope_kernel
        in_specs += [pl.BlockSpec((1, tm, LANES), lambda i, j: (kind_of_tile(j), i, 0))] * 3
        args += list(tables)
    return pl.pallas_call(
        body, grid=(m // tm, n // tn), in_specs=in_specs,
        out_specs=pl.BlockSpec((tm, tn), lambda i, j: (i, j)),
        out_shape=jax.ShapeDtypeStruct((m, n), out_dtype),
        compiler_params=_params(("parallel", "parallel")),
        name="matmul_rope" if rope is not None else "matmul")(*args)


LN_ROWS = 64


def _mm_ln_kernel(y_ref, w_ref, x_ref, g_ref, b_ref, o_ref, obf_ref, *, alpha, nk, n_split):
    k = pl.program_id(1)
    tm, d = o_ref.shape
    dn = d // n_split

    @pl.when(k == 0)
    def _():
        o_ref[...] = alpha * x_ref[...]

    for c in range(n_split):
        cols = slice(c * dn, (c + 1) * dn)
        o_ref[:, cols] += jnp.dot(y_ref[...], w_ref[:, cols], preferred_element_type=jnp.float32)

    @pl.when(k == nk - 1)
    def _():
        def ln_rows(r, carry):
            rows = pl.ds(pl.multiple_of(r * LN_ROWS, LN_ROWS), LN_ROWS)
            h = o_ref[rows, :]
            mu = jnp.mean(h, axis=-1, keepdims=True)
            hc = h - mu
            var = jnp.mean(hc * hc, axis=-1, keepdims=True)
            out = hc * lax.rsqrt(var + LN_EPS) * g_ref[...] + b_ref[...]
            o_ref[rows, :] = out
            obf_ref[rows, :] = out.astype(obf_ref.dtype)
            return carry

        lax.fori_loop(0, tm // LN_ROWS, ln_rows, 0)


def _out_proj_ln(y, w, x, g, b, alpha, tm=512, tk=512):
    m, kdim = y.shape
    d = w.shape[1]
    tm, tk = min(tm, m), min(tk, kdim)
    nk = kdim // tk
    n_split = 2 if d % (2 * LANES) == 0 else 1
    assert m % tm == 0 and kdim % tk == 0 and tm % LN_ROWS == 0
    return pl.pallas_call(
        functools.partial(_mm_ln_kernel, alpha=alpha, nk=nk, n_split=n_split),
        grid=(m // tm, nk),
        in_specs=[pl.BlockSpec((tm, tk), lambda i, k: (i, k)),
                  pl.BlockSpec((tk, d), lambda i, k: (k, 0)),
                  pl.BlockSpec((tm, d), lambda i, k: (i, 0)),
                  pl.BlockSpec((1, d), lambda i, k: (0, 0)),
                  pl.BlockSpec((1, d), lambda i, k: (0, 0))],
        out_specs=[pl.BlockSpec((tm, d), lambda i, k: (i, 0)),
                   pl.BlockSpec((tm, d), lambda i, k: (i, 0))],
        out_shape=[jax.ShapeDtypeStruct((m, d), jnp.float32),
                   jax.ShapeDtypeStruct((m, d), jnp.bfloat16)],
        compiler_params=_params(("parallel", "arbitrary")),
        name="out_proj_ln")(y, w, x, g.reshape(1, d), b.reshape(1, d))


def _ssd_kernel(xr_ref, br_ref, cr_ref, z_ref, dt_ref, dtt_ref,
                cwx_ref, cwb_ref, cwc_ref, cbx_ref, cbb_ref, cbc_ref,
                bias_r_ref, alog_r_ref, bias_c_ref, alog_c_ref, dskip_ref, ng_ref,
                y_ref, xbuf, bbuf, cbuf, acsr_ref, h_ref, *, e_heads):
    g = pl.program_id(0)
    c = pl.program_id(1)
    q = SSD_CHUNK
    gw = xbuf.shape[1]
    hp = dt_ref.shape[1]
    tail = SUBLANES

    @pl.when(c == 0)
    def _():
        xbuf[0:tail, :] = jnp.zeros((tail, gw), jnp.float32)
        bbuf[0:tail, :] = jnp.zeros((tail, SSD_D_STATE), jnp.float32)
        cbuf[0:tail, :] = jnp.zeros((tail, SSD_D_STATE), jnp.float32)
        h_ref[...] = jnp.zeros_like(h_ref)

    def conv_silu(buf, raw_ref, w_ref, b_ref):
        buf[tail:tail + q, :] = raw_ref[...].astype(jnp.float32)
        w = w_ref[...]
        acc = b_ref[...] + w[0:1, :] * buf[tail - 3:tail - 3 + q, :]
        for j in range(1, SSD_CONV):
            acc = acc + w[j:j + 1, :] * buf[tail - 3 + j:tail - 3 + j + q, :]
        buf[0:tail, :] = buf[q:q + tail, :]
        return _silu(acc)

    xs = conv_silu(xbuf, xr_ref, cwx_ref, cbx_ref)
    bm = conv_silu(bbuf, br_ref, cwb_ref, cbb_ref)
    cm = conv_silu(cbuf, cr_ref, cwc_ref, cbc_ref)

    dt_c = _softplus(dt_ref[...] + bias_r_ref[...])
    da_c = dt_c * (-jnp.exp(alog_r_ref[...]))
    dt_r = _softplus(dtt_ref[...] + bias_c_ref[...])
    da_r = dt_r * (-jnp.exp(alog_c_ref[...]))
    ri = lax.broadcasted_iota(jnp.int32, (q, q), 0)
    ci = lax.broadcasted_iota(jnp.int32, (q, q), 1)
    causal = ri >= ci
    tril = causal.astype(jnp.float32)
    triu = (ri <= ci).astype(jnp.float32)
    acs_c = jnp.dot(tril, da_c, precision=_HI, preferred_element_type=jnp.float32)
    acsr_ref[...] = jnp.dot(da_r, triu, precision=_HI, preferred_element_type=jnp.float32)

    eh = lax.broadcasted_iota(jnp.int32, (hp, gw), 0)
    ec = lax.broadcasted_iota(jnp.int32, (hp, gw), 1)
    expand = (eh == g * e_heads + (ec >> SSD_HEAD_SHIFT)).astype(jnp.bfloat16)
    both = jnp.concatenate([dt_c, acs_c], axis=0)
    hi = both.astype(jnp.bfloat16)
    rest = both - hi.astype(jnp.float32)
    mid = rest.astype(jnp.bfloat16)
    low = (rest - mid.astype(jnp.float32)).astype(jnp.bfloat16)
    both_x = jnp.dot(jnp.concatenate([hi, mid, low], axis=1),
                     jnp.concatenate([expand, expand, expand], axis=0),
                     preferred_element_type=jnp.float32)
    dt_x = both_x[:q, :]
    acs_x = both_x[q:, :]
    last_x = acs_x[q - 1:q, :]
    x_dt = xs * dt_x
    x_dt_bf = x_dt.astype(jnp.bfloat16)
    x_dec_bf = (x_dt * jnp.exp(last_x - acs_x)).astype(jnp.bfloat16)

    bt = bm.T
    bt_bf = bt.astype(jnp.bfloat16)
    cm_bf = cm.astype(jnp.bfloat16)
    cb = jnp.dot(cm_bf, bt_bf, preferred_element_type=jnp.float32)

    h_prev = h_ref[...]
    y_off = jnp.dot(cm_bf, h_prev.astype(jnp.bfloat16), preferred_element_type=jnp.float32)
    y = y_off * jnp.exp(acs_x)
    s_new = jnp.dot(bt_bf, x_dec_bf, preferred_element_type=jnp.float32)
    h_ref[...] = jnp.exp(last_x) * h_prev + s_new

    lane = lax.broadcasted_iota(jnp.int32, (q, LANES), 1)
    lo = lane < SSD_HEAD_DIM
    y_parts = []
    for p in range(gw // LANES):
        col = acs_x[:, p * LANES:(p + 1) * LANES]
        col_sw = pltpu.roll(col, SSD_HEAD_DIM, 1)
        xp = x_dt_bf[:, p * LANES:(p + 1) * LANES]
        ws = []
        for half in range(2):
            colb = jnp.where(lo, col, col_sw) if half == 0 else jnp.where(lo, col_sw, col)
            hg = g * e_heads + 2 * p + half
            rowb = acsr_ref[pl.ds(hg, 1), :]
            seg = jnp.where(causal, colb - rowb, NEG_BIG)
            ws.append((cb * jnp.exp(seg)).astype(jnp.bfloat16))
        w2 = jnp.concatenate(ws, axis=1)
        zero = jnp.zeros_like(xp)
        x2 = jnp.concatenate([jnp.where(lo, xp, zero), jnp.where(lo, zero, xp)], axis=0)
        y_parts.append(jnp.dot(w2, x2, preferred_element_type=jnp.float32))
    y = y + jnp.concatenate(y_parts, axis=1)

    y = y + dskip_ref[...] * xs
    y = y * _silu(z_ref[...].astype(jnp.float32))
    ms = jnp.mean(y * y, axis=-1, keepdims=True)
    y_ref[...] = (y * lax.rsqrt(ms + RMS_EPS) * ng_ref[...]).astype(y_ref.dtype)


def _ssd_core(proj, dt_raw, dt_raw_t, conv_w, conv_b, dt_bias, a_log, d_skip, norm_g, d_inner):
    l = proj.shape[0]
    hp = dt_raw.shape[1]
    g_n = SSD_N_GROUPS
    gw = d_inner // g_n
    n = SSD_D_STATE
    e_heads = gw // SSD_HEAD_DIM
    n_heads = d_inner // SSD_HEAD_DIM
    q = SSD_CHUNK
    nc = l // q
    assert l % q == 0 and gw % LANES == 0

    def pad_h(v):
        return jnp.pad(v.astype(jnp.float32), (0, hp - n_heads))

    bias_p, alog_p = pad_h(dt_bias), pad_h(a_log)
    dskip_x = jnp.repeat(d_skip.astype(jnp.float32), SSD_HEAD_DIM).reshape(1, d_inner)
    xb = d_inner // gw
    bb = 2 * d_inner // n
    cbk = (2 * d_inner + g_n * n) // n
    cwb = d_inner // n
    cwc = (d_inner + g_n * n) // n
    conv_b2 = conv_b.reshape(1, -1)

    return pl.pallas_call(
        functools.partial(_ssd_kernel, e_heads=e_heads),
        grid=(g_n, nc),
        in_specs=[
            pl.BlockSpec((q, gw), lambda g, c: (c, xb + g)),
            pl.BlockSpec((q, n), lambda g, c: (c, bb + g)),
            pl.BlockSpec((q, n), lambda g, c: (c, cbk + g)),
            pl.BlockSpec((q, gw), lambda g, c: (c, g)),
            pl.BlockSpec((q, hp), lambda g, c: (c, 0)),
            pl.BlockSpec((hp, q), lambda g, c: (0, c)),
            pl.BlockSpec((SSD_CONV, gw), lambda g, c: (0, g)),
            pl.BlockSpec((SSD_CONV, n), lambda g, c: (0, cwb + g)),
            pl.BlockSpec((SSD_CONV, n), lambda g, c: (0, cwc + g)),
            pl.BlockSpec((1, gw), lambda g, c: (0, g)),
            pl.BlockSpec((1, n), lambda g, c: (0, cwb + g)),
            pl.BlockSpec((1, n), lambda g, c: (0, cwc + g)),
            pl.BlockSpec((1, hp), lambda g, c: (0, 0)),
            pl.BlockSpec((1, hp), lambda g, c: (0, 0)),
            pl.BlockSpec((hp, 1), lambda g, c: (0, 0)),
            pl.BlockSpec((hp, 1), lambda g, c: (0, 0)),
            pl.BlockSpec((1, gw), lambda g, c: (0, g)),
            pl.BlockSpec((1, gw), lambda g, c: (0, g)),
        ],
        out_specs=pl.BlockSpec((q, gw), lambda g, c: (c, g)),
        out_shape=jax.ShapeDtypeStruct((l, d_inner), jnp.bfloat16),
        scratch_shapes=[pltpu.VMEM((q + SUBLANES, gw), jnp.float32),
                        pltpu.VMEM((q + SUBLANES, n), jnp.float32),
                        pltpu.VMEM((q + SUBLANES, n), jnp.float32),
                        pltpu.VMEM((hp, q), jnp.float32),
                        pltpu.VMEM((n, gw), jnp.float32)],
        compiler_params=_params(("parallel", "arbitrary")),
        name="ssd_core")(
            proj, proj, proj, proj, dt_raw, dt_raw_t,
            conv_w, conv_w, conv_w, conv_b2, conv_b2, conv_b2,
            bias_p.reshape(1, hp), alog_p.reshape(1, hp), bias_p.reshape(hp, 1), alog_p.reshape(hp, 1),
            dskip_x, norm_g.astype(jnp.float32).reshape(1, d_inner))


def _idx_small_kernel(a_ref, w_ref, g_ref, b_ref, cos_ref, sa_ref, sb_ref, ki_ref, wi_ref, *, w_scale):
    acc = jnp.dot(a_ref[...], w_ref[...], preferred_element_type=jnp.float32)
    ki = acc[:, :IDX_HEAD_DIM]
    mu = jnp.mean(ki, axis=-1, keepdims=True)
    kc = ki - mu
    var = jnp.mean(kc * kc, axis=-1, keepdims=True)
    kn = kc * lax.rsqrt(var + LN_EPS) * g_ref[...] + b_ref[...]
    _rope_heads(kn, cos_ref[...], sa_ref[...], sb_ref[...], ki_ref)
    wi_ref[...] = acc[:, IDX_HEAD_DIM:] * w_scale


def _idx_small(xb, w_small, kn_g, kn_b, tables, tm=512):
    m, k = xb.shape
    tm = min(tm, m)
    w_scale = IDX_N_HEADS ** -0.5 * IDX_HEAD_DIM ** -0.5
    return pl.pallas_call(
        functools.partial(_idx_small_kernel, w_scale=w_scale),
        grid=(m // tm,),
        in_specs=[pl.BlockSpec((tm, k), lambda i: (i, 0)),
                  pl.BlockSpec((k, 2 * LANES), lambda i: (0, 0)),
                  pl.BlockSpec((1, LANES), lambda i: (0, 0)),
                  pl.BlockSpec((1, LANES), lambda i: (0, 0))]
                 + [pl.BlockSpec((tm, LANES), lambda i: (i, 0))] * 3,
        out_specs=[pl.BlockSpec((tm, LANES), lambda i: (i, 0)),
                   pl.BlockSpec((tm, LANES), lambda i: (i, 0))],
        out_shape=[jax.ShapeDtypeStruct((m, LANES), jnp.bfloat16),
                   jax.ShapeDtypeStruct((m, LANES), jnp.float32)],
        compiler_params=_params(("parallel",)),
        name="idx_small")(xb, w_small, kn_g.reshape(1, -1).astype(jnp.float32),
                          kn_b.reshape(1, -1).astype(jnp.float32), *tables)


SCORE_SUBTILES = 2


def _index_select_kernel(qi_ref, wi_ref, kit_ref, bias_ref, keys_ref, *, tq, tk, ck, topk):
    i = pl.program_id(0)
    l = kit_ref.shape[1]
    n_kt = (((i + 1) * tq + tk * SCORE_SUBTILES - 1) // (tk * SCORE_SUBTILES)) * SCORE_SUBTILES
    q_chunk = (i * tq + lax.broadcasted_iota(jnp.int32, (tq, tk), 0)) >> CHUNK_SHIFT
    wi = wi_ref[...]

    def score_tile(jt):
        k0 = pl.multiple_of(jt * tk, tk)
        kt = kit_ref[:, pl.ds(k0, tk)]
        s = jnp.zeros((tq, tk), jnp.float32)
        for h in range(IDX_N_HEADS):
            d = jnp.dot(qi_ref[:, h * LANES:(h + 1) * LANES], kt, preferred_element_type=jnp.float32)
            s = s + wi[:, h:h + 1] * jnp.maximum(d, 0.0)
        k_chunk = (k0 + lax.broadcasted_iota(jnp.int32, (tq, tk), 1)) >> CHUNK_SHIFT
        bits = lax.bitcast_convert_type(s + 0.0, jnp.int32)
        key = jnp.where(bits < 0, bits ^ jnp.int32(0x7FFFFFFF), bits)
        keys_ref[:, pl.ds(k0, tk)] = jnp.where(k_chunk <= q_chunk, key, jnp.int32(INT_MIN))

    def score_step(it, carry):
        for sub in range(SCORE_SUBTILES):
            score_tile(it * SCORE_SUBTILES + sub)
        return carry

    lax.fori_loop(0, n_kt // SCORE_SUBTILES, score_step, 0)

    n_ct = (n_kt * tk + ck - 1) // ck

    def pad_tile(jt, carry):
        keys_ref[:, pl.ds(pl.multiple_of(jt * tk, tk), tk)] = jnp.full((tq, tk), INT_MIN, jnp.int32)
        return carry

    lax.fori_loop(n_kt, n_ct * (ck // tk), pad_tile, 0)

    def key_to_float(key):
        f = lax.bitcast_convert_type(jnp.where(key < 0, key ^ jnp.int32(0x7FFFFFFF), key), jnp.float32)
        return jnp.where(key == jnp.int32(INT_MIN), -jnp.inf, f)

    def float_to_key(f):
        bits = lax.bitcast_convert_type(f, jnp.int32)
        return jnp.where(bits < 0, bits ^ jnp.int32(0x7FFFFFFF), bits)

    def top2_tile(jt, carry):
        m1, m2 = carry
        blk = keys_ref[:, pl.ds(pl.multiple_of(jt * ck, ck), ck)]
        for cblk in range(ck // LANES):
            x = blk[:, cblk * LANES:(cblk + 1) * LANES]
            m2 = jnp.maximum(m2, jnp.minimum(m1, x))
            m1 = jnp.maximum(m1, x)
        return m1, m2

    none = jnp.full((tq, LANES), INT_MIN, jnp.int32)
    m1, m2 = lax.fori_loop(0, n_ct, top2_tile, (none, none))
    lo_f = jnp.min(key_to_float(m2), axis=1, keepdims=True)
    hi = float_to_key(jnp.max(key_to_float(m1), axis=1, keepdims=True))
    lo = jnp.where(lo_f == -jnp.inf, jnp.int32(INT_MIN), float_to_key(lo_f))

    def count_ge(cand):
        def count_tile(jt, cnt):
            blk = keys_ref[:, pl.ds(pl.multiple_of(jt * ck, ck), ck)]
            ge = jnp.where(blk >= cand, jnp.int32(1), jnp.int32(0))
            for cblk in range(ck // LANES):
                cnt = cnt + ge[:, cblk * LANES:(cblk + 1) * LANES]
            return cnt

        cnt = lax.fori_loop(0, n_ct, count_tile, jnp.zeros((tq, LANES), jnp.int32))
        return jnp.sum(cnt.astype(jnp.float32), axis=1, keepdims=True)

    def open_rows(state):
        lo_, hi_ = state
        return jnp.max(jnp.where(lo_ < hi_, 1.0, 0.0)) > 0.0

    def halve(state):
        lo_, hi_ = state
        x = lo_ ^ hi_
        mid = (lo_ & hi_) + (x >> 1) + (x & 1)
        ok = count_ge(mid) >= float(topk)
        return jnp.where(ok, mid, lo_), jnp.where(ok, hi_, mid - 1)

    t, _ = lax.while_loop(open_rows, halve, (lo, hi))
    t = jnp.maximum(t, jnp.int32(INT_MIN + 1))

    def write_tile(jt, carry):
        k0 = pl.multiple_of(jt * tk, tk)
        sel = keys_ref[:, pl.ds(k0, tk)] >= t
        bias_ref[0, pl.ds(k0, tk), :] = jnp.where(sel, 0.0, NEG_BIG).astype(bias_ref.dtype).T
        return carry

    lax.fori_loop(0, n_kt, write_tile, 0)

    def fill_tile(jt, carry):
        k0 = pl.multiple_of(jt * tk, tk)
        bias_ref[0, pl.ds(k0, tk), :] = jnp.full((tk, tq), NEG_BIG, bias_ref.dtype)
        return carry

    lax.fori_loop(n_kt, l // tk, fill_tile, 0)


def _index_select(proj, wi, ki_t, topk, tq_out, tq=128, tk=256, ck=1024):
    l = proj.shape[0]
    tq, tk, ck = min(tq, l), min(tk, l), min(ck, l)
    qw = IDX_N_HEADS * IDX_HEAD_DIM
    r = tq_out // tq
    assert tq_out % tq == 0 and ck % tk == 0 and l % ck == 0
    assert topk <= 2 * LANES and l % (tk * SCORE_SUBTILES) == 0
    return pl.pallas_call(
        functools.partial(_index_select_kernel, tq=tq, tk=tk, ck=ck, topk=topk),
        grid=(l // tq,),
        in_specs=[pl.BlockSpec((tq, qw), lambda i: (i, 0)),
                  pl.BlockSpec((tq, LANES), lambda i: (i, 0)),
                  pl.BlockSpec((IDX_HEAD_DIM, l), lambda i: (0, 0))],
        out_specs=pl.BlockSpec((1, l, tq), lambda i: (i // r, 0, i % r)),
        out_shape=jax.ShapeDtypeStruct((l // tq_out, l, tq_out), jnp.bfloat16),
        scratch_shapes=[pltpu.VMEM((tq, l), jnp.int32)],
        compiler_params=_params(("parallel",)),
        name="index_select")(proj, wi, ki_t)


ONES_ROWS = 16


def _attn_kernel(qb_ref, kt_ref, qt_ref, z_ref, k_ref, vt_ref, bias_ref, o_ref, m_ref, l_ref, acc_ref,
                 *, tq, tk, group):
    step = pl.program_id(0)
    i = qb_ref[step]
    j = kt_ref[step]
    n_kt = ((i + 1) * tq + tk - 1) // tk

    @pl.when(j == 0)
    def _():
        m_ref[...] = jnp.full_like(m_ref, NEG_BIG)
        l_ref[...] = jnp.zeros_like(l_ref)
        acc_ref[...] = jnp.zeros_like(acc_ref)

    bias = bias_ref[0].astype(jnp.float32)
    bias_g = jnp.concatenate([bias] * group, axis=1)
    ones = jnp.ones((ONES_ROWS, tk), jnp.bfloat16)

    def logits(kv):
        k_g = k_ref[:, kv * LANES:(kv + 1) * LANES]
        qt_g = jnp.concatenate(
            [qt_ref[(kv * group + hh) * LANES:(kv * group + hh + 1) * LANES, :] for hh in range(group)],
            axis=1)
        return jnp.dot(k_g, qt_g, preferred_element_type=jnp.float32) + bias_g

    def softmax(kv, s):
        m_old = m_ref[kv]
        m_new = jnp.maximum(m_old, jnp.max(s, axis=0, keepdims=True))
        m_ref[kv] = m_new
        return jnp.exp2(s - m_new[0:1, :]).astype(jnp.bfloat16), jnp.exp2(m_old - m_new)

    def accumulate(kv, p, alpha):
        v_ext = jnp.concatenate([vt_ref[kv * LANES:(kv + 1) * LANES, :], ones], axis=0)
        pv = jnp.dot(v_ext, p, preferred_element_type=jnp.float32)
        l_ref[kv] = alpha * l_ref[kv] + pv[LANES:LANES + SUBLANES, :]
        for hh in range(group):
            h = kv * group + hh
            acc_ref[h * LANES:(h + 1) * LANES, :] = (
                alpha[0:1, hh * tq:(hh + 1) * tq] * acc_ref[h * LANES:(h + 1) * LANES, :]
                + pv[0:LANES, hh * tq:(hh + 1) * tq])

    s_next = logits(0)
    for kv in range(ATT_N_KV):
        s = s_next
        if kv + 1 < ATT_N_KV:
            s_next = logits(kv + 1)
        p, alpha = softmax(kv, s)
        accumulate(kv, p, alpha)

    @pl.when(j == n_kt - 1)
    def _():
        for h in range(ATT_N_KV * group):
            kv, hh = divmod(h, group)
            l_h = l_ref[kv][0:1, hh * tq:(hh + 1) * tq]
            o_t = acc_ref[h * LANES:(h + 1) * LANES, :] * (1.0 / l_h)
            zg = _silu(z_ref[:, h * LANES:(h + 1) * LANES].astype(jnp.float32))
            o_ref[:, h * LANES:(h + 1) * LANES] = (o_t.T * zg).astype(o_ref.dtype)


def _attention(proj, q_t, v_t, bias_t, z_blk, k_blk, d_att, tq, tk=512):
    l = proj.shape[0]
    tk = min(tk, l)
    n_heads = d_att // ATT_HEAD_DIM
    group = n_heads // ATT_N_KV
    kvw = ATT_N_KV * ATT_HEAD_DIM

    pairs = [(i, j) for i in range(l // tq) for j in range(((i + 1) * tq + tk - 1) // tk)]
    qb = jnp.asarray([p[0] for p in pairs], jnp.int32)
    kt = jnp.asarray([p[1] for p in pairs], jnp.int32)
    grid_spec = pltpu.PrefetchScalarGridSpec(
        num_scalar_prefetch=2,
        grid=(len(pairs),),
        in_specs=[pl.BlockSpec((d_att, tq), lambda s, qb, kt: (0, qb[s])),
                  pl.BlockSpec((tq, d_att), lambda s, qb, kt: (qb[s], z_blk)),
                  pl.BlockSpec((tk, kvw), lambda s, qb, kt: (kt[s], k_blk)),
                  pl.BlockSpec((kvw, tk), lambda s, qb, kt: (0, kt[s])),
                  pl.BlockSpec((1, tk, tq), lambda s, qb, kt: (qb[s], kt[s], 0))],
        out_specs=pl.BlockSpec((tq, d_att), lambda s, qb, kt: (qb[s], 0)),
        scratch_shapes=[pltpu.VMEM((ATT_N_KV, SUBLANES, group * tq), jnp.float32),
                        pltpu.VMEM((ATT_N_KV, SUBLANES, group * tq), jnp.float32),
                        pltpu.VMEM((d_att, tq), jnp.float32)])
    return pl.pallas_call(
        functools.partial(_attn_kernel, tq=tq, tk=tk, group=group),
        grid_spec=grid_spec,
        out_shape=jax.ShapeDtypeStruct((l, d_att), jnp.bfloat16),
        compiler_params=_params(("arbitrary",)),
        name="sparse_attention")(qb, kt, q_t, proj, proj, v_t, bias_t)


def _rope_tables(l):
    half = ROPE_DIM // 2
    inv = jnp.power(ROPE_THETA, -2.0 * jnp.arange(half, dtype=jnp.float32) / ROPE_DIM)
    ang = jnp.arange(l, dtype=jnp.float32)[:, None] * inv[None, :]
    cos, sin = jnp.cos(ang), jnp.sin(ang)
    zeros = jnp.zeros((l, LANES - ROPE_DIM), jnp.float32)
    zh = jnp.zeros((l, half), jnp.float32)
    cos_t = jnp.concatenate([cos, cos, jnp.ones_like(zeros)], axis=1)
    sa_t = jnp.concatenate([-sin, zh, zeros], axis=1)
    sb_t = jnp.concatenate([zh, sin, zeros], axis=1)
    q_scale = ATT_HEAD_DIM ** -0.5 * math.log2(math.e)
    return (jnp.stack([cos_t, cos_t * q_scale, jnp.ones_like(cos_t)]),
            jnp.stack([sa_t, sa_t * q_scale, jnp.zeros_like(sa_t)]),
            jnp.stack([sb_t, sb_t * q_scale, jnp.zeros_like(sb_t)]))


def _ssd_layer(x_f32, x_bf, in_w, conv_w, conv_b, dt_bias, a_log, d_skip, norm_g, out_w, ln_g, ln_b, alpha):
    d = x_bf.shape[1]
    d_inner = out_w.shape[0]
    n_heads = d_inner // SSD_HEAD_DIM
    main_w = 2 * d_inner + 2 * SSD_N_GROUPS * SSD_D_STATE
    hp = -(-n_heads // LANES) * LANES
    w_main = in_w[:, :main_w].astype(jnp.bfloat16)
    w_dt = jnp.pad(in_w[:, main_w:], ((0, 0), (0, hp - n_heads))).astype(jnp.bfloat16)
    proj = _matmul(x_bf, w_main, jnp.bfloat16, 1024, 1024)
    dt_raw = _matmul(x_bf, w_dt, jnp.float32, 1024, LANES)
    y = _ssd_core(proj, dt_raw, dt_raw.T, conv_w, conv_b, dt_bias, a_log, d_skip, norm_g, d_inner)
    return _out_proj_ln(y, out_w.astype(jnp.bfloat16), x_f32, ln_g, ln_b, alpha)


def _dsa_layer(x_f32, x_bf, in_w, kn_g, kn_b, out_w, ln_g, ln_b, alpha, tables):
    l, d = x_bf.shape
    d_att = out_w.shape[0]
    kvw = ATT_N_KV * ATT_HEAD_DIM
    qw = IDX_N_HEADS * IDX_HEAD_DIM
    o_q, o_k, o_v, o_z = 0, d_att, d_att + kvw, d_att + 2 * kvw
    o_qi = o_z + d_att
    o_ki = o_qi + qw
    o_wi = o_ki + IDX_HEAD_DIM
    tn = 1024
    assert d_att % tn == 0 and qw % d_att == 0
    w_main = jnp.concatenate([in_w[:, o_qi:o_ki], in_w[:, o_q:o_k], in_w[:, o_z:o_qi],
                              in_w[:, o_k:o_v], in_w[:, o_v:o_z]], axis=1).astype(jnp.bfloat16)
    w_small = jnp.pad(in_w[:, o_ki:], ((0, 0), (0, 2 * LANES - IDX_HEAD_DIM - IDX_N_HEADS))).astype(jnp.bfloat16)
    c_q, c_z, c_k, c_v = qw, qw + d_att, qw + 2 * d_att, qw + 2 * d_att + kvw
    t_q, t_z, t_k, t_v = c_q // tn, c_z // tn, c_k // tn, c_v // tn

    def kind_of_tile(j):
        return jnp.where(j < t_q, 0, jnp.where(j < t_z, 1, jnp.where(jnp.logical_and(j >= t_k, j < t_v), 0, 2)))

    proj = _matmul(x_bf, w_main, jnp.bfloat16, 1024, tn, rope=(tables, kind_of_tile))
    ki, wi = _idx_small(x_bf, w_small, kn_g, kn_b, tuple(t[0] for t in tables))
    topk = min(TOPK_MAX, l // 4)
    tq_att = min(256, l)
    bias_t = _index_select(proj, wi, ki.T, topk, tq_att)
    q_t = proj[:, c_q:c_q + d_att].T
    v_t = proj[:, c_v:c_v + kvw].T
    o = _attention(proj, q_t, v_t, bias_t, c_z // d_att, c_k // kvw, d_att, tq_att)
    return _out_proj_ln(o, out_w.astype(jnp.bfloat16), x_f32, ln_g, ln_b, alpha)


def kernel(x, ssd_in_w, ssd_conv_w, ssd_conv_b, ssd_dt_bias, ssd_a_log, ssd_d_skip, ssd_norm_g, ssd_out_w,
           dsa_in_w, dsa_kn_g, dsa_kn_b, dsa_out_w, ln_g, ln_b):
    bsz, l, d = x.shape
    depth = ln_g.shape[0]
    alpha = (2 * depth) ** 0.25
    tables = _rope_tables(l)
    outs = []
    for bi in range(bsz):
        xf = x[bi].astype(jnp.float32)
        xb = xf.astype(jnp.bfloat16)
        for i in range(depth):
            j = i // 2
            if i % 2 == 0:
                xf, xb = _ssd_layer(xf, xb, ssd_in_w[j], ssd_conv_w[j], ssd_conv_b[j], ssd_dt_bias[j],
                                    ssd_a_log[j], ssd_d_skip[j], ssd_norm_g[j], ssd_out_w[j],
                                    ln_g[i], ln_b[i], alpha)
            else:
                xf, xb = _dsa_layer(xf, xb, dsa_in_w[j], dsa_kn_g[j], dsa_kn_b[j], dsa_out_w[j],
                                    ln_g[i], ln_b[i], alpha, tables)
        outs.append(xf)
    return jnp.stack(outs).astype(x.dtype)
```

```python
import functools
import math

import jax
import jax.numpy as jnp
from jax import lax
from jax.experimental import pallas as pl
from jax.experimental.pallas import tpu as pltpu

CHUNK = 64
CHUNK_SHIFT = 6
SSD_HEAD_SHIFT = 6
SSD_HEAD_DIM = 64
SSD_D_STATE = 128
SSD_N_GROUPS = 8
SSD_CONV = 4
SSD_CHUNK = 128
ATT_HEAD_DIM = 128
ATT_N_KV = 8
IDX_N_HEADS = 64
IDX_HEAD_DIM = 128
TOPK_MAX = 256
ROPE_THETA = 500000.0
ROPE_DIM = 32
LN_EPS = 1e-5
RMS_EPS = 1e-5

LANES = 128
SUBLANES = 8
V7X_VMEM_BYTES = 64 * 1024 * 1024
VMEM_LIMIT = V7X_VMEM_BYTES - 8 * 1024 * 1024

NEG_BIG = -1e30
INT_MIN = -2 ** 31

_HI = lax.Precision.HIGHEST


def _params(sem):
    return pltpu.CompilerParams(dimension_semantics=sem, vmem_limit_bytes=VMEM_LIMIT)


def _silu(v):
    return v * (1.0 / (1.0 + jnp.exp(-v)))


def _softplus(v):
    return jnp.maximum(v, 0.0) + jnp.log1p(jnp.exp(-jnp.abs(v)))


def _mm_kernel(a_ref, b_ref, o_ref):
    o_ref[...] = jnp.dot(a_ref[...], b_ref[...],
                         preferred_element_type=jnp.float32).astype(o_ref.dtype)


def _rope_heads(acc, cos, sa, sb, o_ref):
    for h in range(acc.shape[1] // LANES):
        a = acc[:, h * LANES:(h + 1) * LANES]
        r = (a * cos + pltpu.roll(a, LANES - ROPE_DIM // 2, 1) * sa
             + pltpu.roll(a, ROPE_DIM // 2, 1) * sb)
        o_ref[:, h * LANES:(h + 1) * LANES] = r.astype(o_ref.dtype)


def _mm_rope_kernel(a_ref, b_ref, cos_ref, sa_ref, sb_ref, o_ref):
    acc = jnp.dot(a_ref[...], b_ref[...], preferred_element_type=jnp.float32)
    _rope_heads(acc, cos_ref[0], sa_ref[0], sb_ref[0], o_ref)


def _matmul(a, b, out_dtype, tm, tn, rope=None):
    m, k = a.shape
    n = b.shape[1]
    tm, tn = min(tm, m), min(tn, n)
    assert m % tm == 0 and n % tn == 0
    in_specs = [pl.BlockSpec((tm, k), lambda i, j: (i, 0)),
                pl.BlockSpec((k, tn), lambda i, j: (0, j))]
    args = [a, b]
    if rope is None:
        body = _mm_kernel
    else:
        tables, kind_of_tile = rope
        body = _mm_rope_kernel
        in_specs += [pl.BlockSpec((1, tm, LANES), lambda i, j: (kind_of_tile(j), i, 0))] * 3
        args += list(tables)
    return pl.pallas_call(
        body, grid=(m // tm, n // tn), in_specs=in_specs,
        out_specs=pl.BlockSpec((tm, tn), lambda i, j: (i, j)),
        out_shape=jax.ShapeDtypeStruct((m, n), out_dtype),
        compiler_params=_params(("parallel", "parallel")),
        name="matmul_rope" if rope is not None else "matmul")(*args)


LN_ROWS = 64


def _mm_ln_kernel(y_ref, w_ref, x_ref, g_ref, b_ref, o_ref, obf_ref, *, alpha, nk, n_split):
    k = pl.program_id(1)
    tm, d = o_ref.shape
    dn = d // n_split

    @pl.when(k == 0)
    def _():
        o_ref[...] = alpha * x_ref[...]

    for c in range(n_split):
        cols = slice(c * dn, (c + 1) * dn)
        o_ref[:, cols] += jnp.dot(y_ref[...], w_ref[:, cols], preferred_element_type=jnp.float32)

    @pl.when(k == nk - 1)
    def _():
        def ln_rows(r, carry):
            rows = pl.ds(pl.multiple_of(r * LN_ROWS, LN_ROWS), LN_ROWS)
            h = o_ref[rows, :]
            mu = jnp.mean(h, axis=-1, keepdims=True)
            hc = h - mu
            var = jnp.mean(hc * hc, axis=-1, keepdims=True)
            out = hc * lax.rsqrt(var + LN_EPS) * g_ref[...] + b_ref[...]
            o_ref[rows, :] = out
            obf_ref[rows, :] = out.astype(obf_ref.dtype)
            return carry

        lax.fori_loop(0, tm // LN_ROWS, ln_rows, 0)


def _out_proj_ln(y, w, x, g, b, alpha, tm=512, tk=512):
    m, kdim = y.shape
    d = w.shape[1]
    tm, tk = min(tm, m), min(tk, kdim)
    nk = kdim // tk
    n_split = 2 if d % (2 * LANES) == 0 else 1
    assert m % tm == 0 and kdim % tk == 0 and tm % LN_ROWS == 0
    return pl.pallas_call(
        functools.partial(_mm_ln_kernel, alpha=alpha, nk=nk, n_split=n_split),
        grid=(m // tm, nk),
        in_specs=[pl.BlockSpec((tm, tk), lambda i, k: (i, k)),
                  pl.BlockSpec((tk, d), lambda i, k: (k, 0)),
                  pl.BlockSpec((tm, d), lambda i, k: (i, 0)),
                  pl.BlockSpec((1, d), lambda i, k: (0, 0)),
                  pl.BlockSpec((1, d), lambda i, k: (0, 0))],
        out_specs=[pl.BlockSpec((tm, d), lambda i, k: (i, 0)),
                   pl.BlockSpec((tm, d), lambda i, k: (i, 0))],
        out_shape=[jax.ShapeDtypeStruct((m, d), jnp.float32),
                   jax.ShapeDtypeStruct((m, d), jnp.bfloat16)],
        compiler_params=_params(("parallel", "arbitrary")),
        name="out_proj_ln")(y, w, x, g.reshape(1, d), b.reshape(1, d))


CONV_TAIL = 16


def _ssd_kernel(xr_ref, br_ref, cr_ref, z_ref, dt_ref, dtt_ref,
                cwx_ref, cwb_ref, cwc_ref, cbx_ref, cbb_ref, cbc_ref,
                bias_r_ref, alog_r_ref, bias_c_ref, alog_c_ref, dskip_ref, ng_ref,
                y_ref, xbuf, bbuf, cbuf, acsr_ref, h_ref, *, e_heads):
    g = pl.program_id(0)
    c = pl.program_id(1)
    q = SSD_CHUNK
    gw = xbuf.shape[1]
    hp = dt_ref.shape[1]
    tail = CONV_TAIL

    @pl.when(c == 0)
    def _():
        xbuf[...] = jnp.zeros_like(xbuf)
        bbuf[...] = jnp.zeros_like(bbuf)
        cbuf[...] = jnp.zeros_like(cbuf)
        h_ref[...] = jnp.zeros_like(h_ref)

    sr = lax.broadcasted_iota(jnp.int32, (q, tail + q), 0)
    sc = lax.broadcasted_iota(jnp.int32, (q, tail + q), 1)

    def conv_silu(buf, raw_ref, w_ref, b_ref):
        raw = raw_ref[...]
        both = jnp.concatenate([buf[...], raw], axis=0)
        w = w_ref[...]
        acc = b_ref[...] + w[SSD_CONV - 1:SSD_CONV, :] * raw.astype(jnp.float32)
        for j in range(SSD_CONV - 1):
            shift = (sc == sr + (tail - (SSD_CONV - 1) + j)).astype(jnp.bfloat16)
            acc = acc + w[j:j + 1, :] * jnp.dot(shift, both, preferred_element_type=jnp.float32)
        buf[...] = raw[q - tail:, :]
        return _silu(acc)

    xs = conv_silu(xbuf, xr_ref, cwx_ref, cbx_ref)
    bm = conv_silu(bbuf, br_ref, cwb_ref, cbb_ref)
    cm = conv_silu(cbuf, cr_ref, cwc_ref, cbc_ref)

    dt_c = _softplus(dt_ref[...] + bias_r_ref[...])
    da_c = dt_c * (-jnp.exp(alog_r_ref[...]))
    dt_r = _softplus(dtt_ref[...] + bias_c_ref[...])
    da_r = dt_r * (-jnp.exp(alog_c_ref[...]))
    ri = lax.broadcasted_iota(jnp.int32, (q, q), 0)
    ci = lax.broadcasted_iota(jnp.int32, (q, q), 1)
    causal = ri >= ci
    tril = causal.astype(jnp.float32)
    triu = (ri <= ci).astype(jnp.float32)
    acs_c = jnp.dot(tril, da_c, precision=_HI, preferred_element_type=jnp.float32)
    acsr_ref[...] = jnp.dot(da_r, triu, precision=_HI, preferred_element_type=jnp.float32)

    eh = lax.broadcasted_iota(jnp.int32, (hp, gw), 0)
    ec = lax.broadcasted_iota(jnp.int32, (hp, gw), 1)
    expand = (eh == g * e_heads + (ec >> SSD_HEAD_SHIFT)).astype(jnp.bfloat16)
    both = jnp.concatenate([dt_c, acs_c], axis=0)
    hi = both.astype(jnp.bfloat16)
    rest = both - hi.astype(jnp.float32)
    mid = rest.astype(jnp.bfloat16)
    low = (rest - mid.astype(jnp.float32)).astype(jnp.bfloat16)
    both_x = jnp.dot(jnp.concatenate([hi, mid, low], axis=1),
                     jnp.concatenate([expand, expand, expand], axis=0),
                     preferred_element_type=jnp.float32)
    dt_x = both_x[:q, :]
    acs_x = both_x[q:, :]
    last_x = acs_x[q - 1:q, :]
    x_dt = xs * dt_x
    x_dt_bf = x_dt.astype(jnp.bfloat16)
    x_dec_bf = (x_dt * jnp.exp(last_x - acs_x)).astype(jnp.bfloat16)

    bt = bm.T
    bt_bf = bt.astype(jnp.bfloat16)
    cm_bf = cm.astype(jnp.bfloat16)
    cb = jnp.dot(cm_bf, bt_bf, preferred_element_type=jnp.float32)

    h_prev = h_ref[...]
    y_off = jnp.dot(cm_bf, h_prev.astype(jnp.bfloat16), preferred_element_type=jnp.float32)
    y = y_off * jnp.exp(acs_x)
    s_new = jnp.dot(bt_bf, x_dec_bf, preferred_element_type=jnp.float32)
    h_ref[...] = jnp.exp(last_x) * h_prev + s_new

    lane = lax.broadcasted_iota(jnp.int32, (q, LANES), 1)
    lo = lane < SSD_HEAD_DIM
    y_parts = []
    for p in range(gw // LANES):
        col = acs_x[:, p * LANES:(p + 1) * LANES]
        col_sw = pltpu.roll(col, SSD_HEAD_DIM, 1)
        xp = x_dt_bf[:, p * LANES:(p + 1) * LANES]
        ws = []
        for half in range(2):
            colb = jnp.where(lo, col, col_sw) if half == 0 else jnp.where(lo, col_sw, col)
            hg = g * e_heads + 2 * p + half
            rowb = acsr_ref[pl.ds(hg, 1), :]
            seg = jnp.where(causal, colb - rowb, NEG_BIG)
            ws.append((cb * jnp.exp(seg)).astype(jnp.bfloat16))
        w2 = jnp.concatenate(ws, axis=1)
        zero = jnp.zeros_like(xp)
        x2 = jnp.concatenate([jnp.where(lo, xp, zero), jnp.where(lo, zero, xp)], axis=0)
        y_parts.append(jnp.dot(w2, x2, preferred_element_type=jnp.float32))
    y = y + jnp.concatenate(y_parts, axis=1)

    y = y + dskip_ref[...] * xs
    y = y * _silu(z_ref[...].astype(jnp.float32))
    ms = jnp.mean(y * y, axis=-1, keepdims=True)
    y_ref[...] = (y * lax.rsqrt(ms + RMS_EPS) * ng_ref[...]).astype(y_ref.dtype)


def _ssd_core(proj, dt_raw, dt_raw_t, conv_w, conv_b, dt_bias, a_log, d_skip, norm_g, d_inner):
    l = proj.shape[0]
    hp = dt_raw.shape[1]
    g_n = SSD_N_GROUPS
    gw = d_inner // g_n
    n = SSD_D_STATE
    e_heads = gw // SSD_HEAD_DIM
    n_heads = d_inner // SSD_HEAD_DIM
    q = SSD_CHUNK
    nc = l // q
    assert l % q == 0 and gw % LANES == 0

    def pad_h(v):
        return jnp.pad(v.astype(jnp.float32), (0, hp - n_heads))

    bias_p, alog_p = pad_h(dt_bias), pad_h(a_log)
    dskip_x = jnp.repeat(d_skip.astype(jnp.float32), SSD_HEAD_DIM).reshape(1, d_inner)
    xb = d_inner // gw
    bb = 2 * d_inner // n
    cbk = (2 * d_inner + g_n * n) // n
    cwb = d_inner // n
    cwc = (d_inner + g_n * n) // n
    conv_b2 = conv_b.reshape(1, -1)

    return pl.pallas_call(
        functools.partial(_ssd_kernel, e_heads=e_heads),
        grid=(g_n, nc),
        in_specs=[
            pl.BlockSpec((q, gw), lambda g, c: (c, xb + g)),
            pl.BlockSpec((q, n), lambda g, c: (c, bb + g)),
            pl.BlockSpec((q, n), lambda g, c: (c, cbk + g)),
            pl.BlockSpec((q, gw), lambda g, c: (c, g)),
            pl.BlockSpec((q, hp), lambda g, c: (c, 0)),
            pl.BlockSpec((hp, q), lambda g, c: (0, c)),
            pl.BlockSpec((SSD_CONV, gw), lambda g, c: (0, g)),
            pl.BlockSpec((SSD_CONV, n), lambda g, c: (0, cwb + g)),
            pl.BlockSpec((SSD_CONV, n), lambda g, c: (0, cwc + g)),
            pl.BlockSpec((1, gw), lambda g, c: (0, g)),
            pl.BlockSpec((1, n), lambda g, c: (0, cwb + g)),
            pl.BlockSpec((1, n), lambda g, c: (0, cwc + g)),
            pl.BlockSpec((1, hp), lambda g, c: (0, 0)),
            pl.BlockSpec((1, hp), lambda g, c: (0, 0)),
            pl.BlockSpec((hp, 1), lambda g, c: (0, 0)),
            pl.BlockSpec((hp, 1), lambda g, c: (0, 0)),
            pl.BlockSpec((1, gw), lambda g, c: (0, g)),
            pl.BlockSpec((1, gw), lambda g, c: (0, g)),
        ],
        out_specs=pl.BlockSpec((q, gw), lambda g, c: (c, g)),
        out_shape=jax.ShapeDtypeStruct((l, d_inner), jnp.bfloat16),
        scratch_shapes=[pltpu.VMEM((CONV_TAIL, gw), jnp.bfloat16),
                        pltpu.VMEM((CONV_TAIL, n), jnp.bfloat16),
                        pltpu.VMEM((CONV_TAIL, n), jnp.bfloat16),
                        pltpu.VMEM((hp, q), jnp.float32),
                        pltpu.VMEM((n, gw), jnp.float32)],
        compiler_params=_params(("parallel", "arbitrary")),
        name="ssd_core")(
            proj, proj, proj, proj, dt_raw, dt_raw_t,
            conv_w, conv_w, conv_w, conv_b2, conv_b2, conv_b2,
            bias_p.reshape(1, hp), alog_p.reshape(1, hp), bias_p.reshape(hp, 1), alog_p.reshape(hp, 1),
            dskip_x, norm_g.astype(jnp.float32).reshape(1, d_inner))


def _idx_small_kernel(a_ref, w_ref, g_ref, b_ref, cos_ref, sa_ref, sb_ref, ki_ref, wi_ref, *, w_scale):
    acc = jnp.dot(a_ref[...], w_ref[...], preferred_element_type=jnp.float32)
    ki = acc[:, :IDX_HEAD_DIM]
    mu = jnp.mean(ki, axis=-1, keepdims=True)
    kc = ki - mu
    var = jnp.mean(kc * kc, axis=-1, keepdims=True)
    kn = kc * lax.rsqrt(var + LN_EPS) * g_ref[...] + b_ref[...]
    _rope_heads(kn, cos_ref[...], sa_ref[...], sb_ref[...], ki_ref)
    wi_ref[...] = acc[:, IDX_HEAD_DIM:] * w_scale


def _idx_small(xb, w_small, kn_g, kn_b, tables, tm=512):
    m, k = xb.shape
    tm = min(tm, m)
    w_scale = IDX_N_HEADS ** -0.5 * IDX_HEAD_DIM ** -0.5
    return pl.pallas_call(
        functools.partial(_idx_small_kernel, w_scale=w_scale),
        grid=(m // tm,),
        in_specs=[pl.BlockSpec((tm, k), lambda i: (i, 0)),
                  pl.BlockSpec((k, 2 * LANES), lambda i: (0, 0)),
                  pl.BlockSpec((1, LANES), lambda i: (0, 0)),
                  pl.BlockSpec((1, LANES), lambda i: (0, 0))]
                 + [pl.BlockSpec((tm, LANES), lambda i: (i, 0))] * 3,
        out_specs=[pl.BlockSpec((tm, LANES), lambda i: (i, 0)),
                   pl.BlockSpec((tm, LANES), lambda i: (i, 0))],
        out_shape=[jax.ShapeDtypeStruct((m, LANES), jnp.bfloat16),
                   jax.ShapeDtypeStruct((m, LANES), jnp.float32)],
        compiler_params=_params(("parallel",)),
        name="idx_small")(xb, w_small, kn_g.reshape(1, -1).astype(jnp.float32),
                          kn_b.reshape(1, -1).astype(jnp.float32), *tables)


SCORE_SUBTILES = 2


def _index_select_kernel(qi_ref, wi_ref, kit_ref, bias_ref, keys_ref, *, tq, tk, ck, topk):
    i = pl.program_id(0)
    l = kit_ref.shape[1]
    n_kt = (((i + 1) * tq + tk * SCORE_SUBTILES - 1) // (tk * SCORE_SUBTILES)) * SCORE_SUBTILES
    q_chunk = (i * tq + lax.broadcasted_iota(jnp.int32, (tq, tk), 0)) >> CHUNK_SHIFT
    wi = wi_ref[...]

    def score_tile(jt):
        k0 = pl.multiple_of(jt * tk, tk)
        kt = kit_ref[:, pl.ds(k0, tk)]
        s = jnp.zeros((tq, tk), jnp.float32)
        for h in range(IDX_N_HEADS):
            d = jnp.dot(qi_ref[:, h * LANES:(h + 1) * LANES], kt, preferred_element_type=jnp.float32)
            s = s + wi[:, h:h + 1] * jnp.maximum(d, 0.0)
        k_chunk = (k0 + lax.broadcasted_iota(jnp.int32, (tq, tk), 1)) >> CHUNK_SHIFT
        bits = lax.bitcast_convert_type(s + 0.0, jnp.int32)
        key = jnp.where(bits < 0, bits ^ jnp.int32(0x7FFFFFFF), bits)
        keys_ref[:, pl.ds(k0, tk)] = jnp.where(k_chunk <= q_chunk, key, jnp.int32(INT_MIN))

    def score_step(it, carry):
        for sub in range(SCORE_SUBTILES):
            score_tile(it * SCORE_SUBTILES + sub)
        return carry

    lax.fori_loop(0, n_kt // SCORE_SUBTILES, score_step, 0)

    n_ct = (n_kt * tk + ck - 1) // ck

    def pad_tile(jt, carry):
        keys_ref[:, pl.ds(pl.multiple_of(jt * tk, tk), tk)] = jnp.full((tq, tk), INT_MIN, jnp.int32)
        return carry

    lax.fori_loop(n_kt, n_ct * (ck // tk), pad_tile, 0)

    def key_to_float(key):
        f = lax.bitcast_convert_type(jnp.where(key < 0, key ^ jnp.int32(0x7FFFFFFF), key), jnp.float32)
        return jnp.where(key == jnp.int32(INT_MIN), -jnp.inf, f)

    def float_to_key(f):
        bits = lax.bitcast_convert_type(f, jnp.int32)
        return jnp.where(bits < 0, bits ^ jnp.int32(0x7FFFFFFF), bits)

    def top2_tile(jt, carry):
        m1, m2 = carry
        blk = keys_ref[:, pl.ds(pl.multiple_of(jt * ck, ck), ck)]
        for cblk in range(ck // LANES):
            x = blk[:, cblk * LANES:(cblk + 1) * LANES]
            m2 = jnp.maximum(m2, jnp.minimum(m1, x))
            m1 = jnp.maximum(m1, x)
        return m1, m2

    none = jnp.full((tq, LANES), INT_MIN, jnp.int32)
    m1, m2 = lax.fori_loop(0, n_ct, top2_tile, (none, none))
    lo_f = jnp.min(key_to_float(m2), axis=1, keepdims=True)
    hi = float_to_key(jnp.max(key_to_float(m1), axis=1, keepdims=True))
    lo = jnp.where(lo_f == -jnp.inf, jnp.int32(INT_MIN), float_to_key(lo_f))

    def count_ge(cand):
        def count_tile(jt, cnt):
            blk = keys_ref[:, pl.ds(pl.multiple_of(jt * ck, ck), ck)]
            ge = jnp.where(blk >= cand, jnp.int32(1), jnp.int32(0))
            for cblk in range(ck // LANES):
                cnt = cnt + ge[:, cblk * LANES:(cblk + 1) * LANES]
            return cnt

        cnt = lax.fori_loop(0, n_ct, count_tile, jnp.zeros((tq, LANES), jnp.int32))
        return jnp.sum(cnt.astype(jnp.float32), axis=1, keepdims=True)

    def open_rows(state):
        lo_, hi_ = state
        return jnp.max(jnp.where(lo_ < hi_, 1.0, 0.0)) > 0.0

    def halve(state):
        lo_, hi_ = state
        x = lo_ ^ hi_
        mid = (lo_ & hi_) + (x >> 1) + (x & 1)
        ok = count_ge(mid) >= float(topk)
        return jnp.where(ok, mid, lo_), jnp.where(ok, hi_, mid - 1)

    t, _ = lax.while_loop(open_rows, halve, (lo, hi))
    t = jnp.maximum(t, jnp.int32(INT_MIN + 1))

    def write_tile(jt, carry):
        k0 = pl.multiple_of(jt * tk, tk)
        sel = keys_ref[:, pl.ds(k0, tk)] >= t
        bias_ref[0, pl.ds(k0, tk), :] = jnp.where(sel, 0.0, NEG_BIG).astype(bias_ref.dtype).T
        return carry

    lax.fori_loop(0, n_kt, write_tile, 0)

    def fill_tile(jt, carry):
        k0 = pl.multiple_of(jt * tk, tk)
        bias_ref[0, pl.ds(k0, tk), :] = jnp.full((tk, tq), NEG_BIG, bias_ref.dtype)
        return carry

    lax.fori_loop(n_kt, l // tk, fill_tile, 0)


def _index_select(proj, wi, ki_t, topk, tq_out, tq=128, tk=256, ck=1024):
    l = proj.shape[0]
    tq, tk, ck = min(tq, l), min(tk, l), min(ck, l)
    qw = IDX_N_HEADS * IDX_HEAD_DIM
    r = tq_out // tq
    assert tq_out % tq == 0 and ck % tk == 0 and l % ck == 0
    assert topk <= 2 * LANES and l % (tk * SCORE_SUBTILES) == 0
    return pl.pallas_call(
        functools.partial(_index_select_kernel, tq=tq, tk=tk, ck=ck, topk=topk),
        grid=(l // tq,),
        in_specs=[pl.BlockSpec((tq, qw), lambda i: (i, 0)),
                  pl.BlockSpec((tq, LANES), lambda i: (i, 0)),
                  pl.BlockSpec((IDX_HEAD_DIM, l), lambda i: (0, 0))],
        out_specs=pl.BlockSpec((1, l, tq), lambda i: (i // r, 0, i % r)),
        out_shape=jax.ShapeDtypeStruct((l // tq_out, l, tq_out), jnp.bfloat16),
        scratch_shapes=[pltpu.VMEM((tq, l), jnp.int32)],
        compiler_params=_params(("parallel",)),
        name="index_select")(proj, wi, ki_t)


ONES_ROWS = 16


def _attn_kernel(qb_ref, kt_ref, qt_ref, z_ref, k_ref, vt_ref, bias_ref, o_ref, m_ref, l_ref, acc_ref,
                 *, tq, tk, group):
    step = pl.program_id(0)
    i = qb_ref[step]
    j = kt_ref[step]
    n_kt = ((i + 1) * tq + tk - 1) // tk

    @pl.when(j == 0)
    def _():
        m_ref[...] = jnp.full_like(m_ref, NEG_BIG)
        l_ref[...] = jnp.zeros_like(l_ref)
        acc_ref[...] = jnp.zeros_like(acc_ref)

    bias = bias_ref[0].astype(jnp.float32)
    bias_g = jnp.concatenate([bias] * group, axis=1)
    ones = jnp.ones((ONES_ROWS, tk), jnp.bfloat16)

    def logits(kv):
        k_g = k_ref[:, kv * LANES:(kv + 1) * LANES]
        qt_g = jnp.concatenate(
            [qt_ref[(kv * group + hh) * LANES:(kv * group + hh + 1) * LANES, :] for hh in range(group)],
            axis=1)
        return jnp.dot(k_g, qt_g, preferred_element_type=jnp.float32) + bias_g

    def softmax(kv, s):
        m_old = m_ref[kv]
        m_new = jnp.maximum(m_old, jnp.max(s, axis=0, keepdims=True))
        m_ref[kv] = m_new
        return jnp.exp2(s - m_new[0:1, :]).astype(jnp.bfloat16), jnp.exp2(m_old - m_new)

    def accumulate(kv, p, alpha):
        v_ext = jnp.concatenate([vt_ref[kv * LANES:(kv + 1) * LANES, :], ones], axis=0)
        pv = jnp.dot(v_ext, p, preferred_element_type=jnp.float32)
        l_ref[kv] = alpha * l_ref[kv] + pv[LANES:LANES + SUBLANES, :]
        for hh in range(group):
            h = kv * group + hh
            acc_ref[h * LANES:(h + 1) * LANES, :] = (
                alpha[0:1, hh * tq:(hh + 1) * tq] * acc_ref[h * LANES:(h + 1) * LANES, :]
                + pv[0:LANES, hh * tq:(hh + 1) * tq])

    s_next = logits(0)
    for kv in range(ATT_N_KV):
        s = s_next
        if kv + 1 < ATT_N_KV:
            s_next = logits(kv + 1)
        p, alpha = softmax(kv, s)
        accumulate(kv, p, alpha)

    @pl.when(j == n_kt - 1)
    def _():
        for h in range(ATT_N_KV * group):
            kv, hh = divmod(h, group)
            l_h = l_ref[kv][0:1, hh * tq:(hh + 1) * tq]
            o_t = acc_ref[h * LANES:(h + 1) * LANES, :] * (1.0 / l_h)
            zg = _silu(z_ref[:, h * LANES:(h + 1) * LANES].astype(jnp.float32))
            o_ref[:, h * LANES:(h + 1) * LANES] = (o_t.T * zg).astype(o_ref.dtype)


def _attention(proj, q_t, v_t, bias_t, z_blk, k_blk, d_att, tq, tk=512):
    l = proj.shape[0]
    tk = min(tk, l)
    n_heads = d_att // ATT_HEAD_DIM
    group = n_heads // ATT_N_KV
    kvw = ATT_N_KV * ATT_HEAD_DIM

    pairs = [(i, j) for i in range(l // tq) for j in range(((i + 1) * tq + tk - 1) // tk)]
    qb = jnp.asarray([p[0] for p in pairs], jnp.int32)
    kt = jnp.asarray([p[1] for p in pairs], jnp.int32)
    grid_spec = pltpu.PrefetchScalarGridSpec(
        num_scalar_prefetch=2,
        grid=(len(pairs),),
        in_specs=[pl.BlockSpec((d_att, tq), lambda s, qb, kt: (0, qb[s])),
                  pl.BlockSpec((tq, d_att), lambda s, qb, kt: (qb[s], z_blk)),
                  pl.BlockSpec((tk, kvw), lambda s, qb, kt: (kt[s], k_blk)),
                  pl.BlockSpec((kvw, tk), lambda s, qb, kt: (0, kt[s])),
                  pl.BlockSpec((1, tk, tq), lambda s, qb, kt: (qb[s], kt[s], 0))],
        out_specs=pl.BlockSpec((tq, d_att), lambda s, qb, kt: (qb[s], 0)),
        scratch_shapes=[pltpu.VMEM((ATT_N_KV, SUBLANES, group * tq), jnp.float32),
                        pltpu.VMEM((ATT_N_KV, SUBLANES, group * tq), jnp.float32),
                        pltpu.VMEM((d_att, tq), jnp.float32)])
    return pl.pallas_call(
        functools.partial(_attn_kernel, tq=tq, tk=tk, group=group),
        grid_spec=grid_spec,
        out_shape=jax.ShapeDtypeStruct((l, d_att), jnp.bfloat16),
        compiler_params=_params(("arbitrary",)),
        name="sparse_attention")(qb, kt, q_t, proj, proj, v_t, bias_t)


def _rope_tables(l):
    half = ROPE_DIM // 2
    inv = jnp.power(ROPE_THETA, -2.0 * jnp.arange(half, dtype=jnp.float32) / ROPE_DIM)
    ang = jnp.arange(l, dtype=jnp.float32)[:, None] * inv[None, :]
    cos, sin = jnp.cos(ang), jnp.sin(ang)
    zeros = jnp.zeros((l, LANES - ROPE_DIM), jnp.float32)
    zh = jnp.zeros((l, half), jnp.float32)
    cos_t = jnp.concatenate([cos, cos, jnp.ones_like(zeros)], axis=1)
    sa_t = jnp.concatenate([-sin, zh, zeros], axis=1)
    sb_t = jnp.concatenate([zh, sin, zeros], axis=1)
    q_scale = ATT_HEAD_DIM ** -0.5 * math.log2(math.e)
    return (jnp.stack([cos_t, cos_t * q_scale, jnp.ones_like(cos_t)]),
            jnp.stack([sa_t, sa_t * q_scale, jnp.zeros_like(sa_t)]),
            jnp.stack([sb_t, sb_t * q_scale, jnp.zeros_like(sb_t)]))


def _ssd_layer(x_f32, x_bf, in_w, conv_w, conv_b, dt_bias, a_log, d_skip, norm_g, out_w, ln_g, ln_b, alpha):
    d = x_bf.shape[1]
    d_inner = out_w.shape[0]
    n_heads = d_inner // SSD_HEAD_DIM
    main_w = 2 * d_inner + 2 * SSD_N_GROUPS * SSD_D_STATE
    hp = -(-n_heads // LANES) * LANES
    w_main = in_w[:, :main_w].astype(jnp.bfloat16)
    w_dt = jnp.pad(in_w[:, main_w:], ((0, 0), (0, hp - n_heads))).astype(jnp.bfloat16)
    proj = _matmul(x_bf, w_main, jnp.bfloat16, 1024, 1024)
    dt_raw = _matmul(x_bf, w_dt, jnp.float32, 1024, LANES)
    y = _ssd_core(proj, dt_raw, dt_raw.T, conv_w, conv_b, dt_bias, a_log, d_skip, norm_g, d_inner)
    return _out_proj_ln(y, out_w.astype(jnp.bfloat16), x_f32, ln_g, ln_b, alpha)


def _dsa_layer(x_f32, x_bf, in_w, kn_g, kn_b, out_w, ln_g, ln_b, alpha, tables):
    l, d = x_bf.shape
    d_att = out_w.shape[0]
    kvw = ATT_N_KV * ATT_HEAD_DIM
    qw = IDX_N_HEADS * IDX_HEAD_DIM
    o_q, o_k, o_v, o_z = 0, d_att, d_att + kvw, d_att + 2 * kvw
    o_qi = o_z + d_att
    o_ki = o_qi + qw
    o_wi = o_ki + IDX_HEAD_DIM
    tn = 1024
    assert d_att % tn == 0 and qw % d_att == 0
    w_main = jnp.concatenate([in_w[:, o_qi:o_ki], in_w[:, o_q:o_k], in_w[:, o_z:o_qi],
                              in_w[:, o_k:o_v], in_w[:, o_v:o_z]], axis=1).astype(jnp.bfloat16)
    w_small = jnp.pad(in_w[:, o_ki:], ((0, 0), (0, 2 * LANES - IDX_HEAD_DIM - IDX_N_HEADS))).astype(jnp.bfloat16)
    c_q, c_z, c_k, c_v = qw, qw + d_att, qw + 2 * d_att, qw + 2 * d_att + kvw
    t_q, t_z, t_k, t_v = c_q // tn, c_z // tn, c_k // tn, c_v // tn

    def kind_of_tile(j):
        return jnp.where(j < t_q, 0, jnp.where(j < t_z, 1, jnp.where(jnp.logical_and(j >= t_k, j < t_v), 0, 2)))

    proj = _matmul(x_bf, w_main, jnp.bfloat16, 1024, tn, rope=(tables, kind_of_tile))
    ki, wi = _idx_small(x_bf, w_small, kn_g, kn_b, tuple(t[0] for t in tables))
    topk = min(TOPK_MAX, l // 4)
    tq_att = min(256, l)
    bias_t = _index_select(proj, wi, ki.T, topk, tq_att)
    q_t = proj[:, c_q:c_q + d_att].T
    v_t = proj[:, c_v:c_v + kvw].T
    o = _attention(proj, q_t, v_t, bias_t, c_z // d_att, c_k // kvw, d_att, tq_att)
    return _out_proj_ln(o, out_w.astype(jnp.bfloat16), x_f32, ln_g, ln_b, alpha)


def kernel(x, ssd_in_w, ssd_conv_w, ssd_conv_b, ssd_dt_bias, ssd_a_log, ssd_d_skip, ssd_norm_g, ssd_out_w,
           dsa_in_w, dsa_kn_g, dsa_kn_b, dsa_out_w, ln_g, ln_b):
    bsz, l, d = x.shape
    depth = ln_g.shape[0]
    alpha = (2 * depth) ** 0.25
    tables = _rope_tables(l)
    outs = []
    for bi in range(bsz):
        xf = x[bi].astype(jnp.float32)
        xb = xf.astype(jnp.bfloat16)
        for i in range(depth):
            j = i // 2
            if i % 2 == 0:
                xf, xb = _ssd_layer(xf, xb, ssd_in_w[j], ssd_conv_w[j], ssd_conv_b[j], ssd_dt_bias[j],
                                    ssd_a_log[j], ssd_d_skip[j], ssd_norm_g[j], ssd_out_w[j],
                                    ln_g[i], ln_b[i], alpha)
            else:
                xf, xb = _dsa_layer(xf, xb, dsa_in_w[j], dsa_kn_g[j], dsa_kn_b[j], dsa_out_w[j],
                                    ln_g[i], ln_b[i], alpha, tables)
        outs.append(xf)
    return jnp.stack(outs).astype(x.dtype)
```
